```python
import math
import jax
import jax.numpy as jnp
from jax import lax
import numpy as np

D_MODEL = 1024
BATCH = 2
SEQ = 8192
DEPTH = 2

GRID_W = 64
ROPE_THETA = 10000.0
Q_BLOCK = 128
NORM_EPS = 1e-6

BRANCH = D_MODEL // 2
D_MIX = 4 * BRANCH

SSM_HEAD = 64
SSM_HEADS = BRANCH // SSM_HEAD
SSM_GROUPS = 2
SSM_STATE = 128
SSM_CHUNK = 128
D_CONV = 5
SSM_CONV_CH = BRANCH + 2 * SSM_GROUPS * SSM_STATE

RWKV_HEAD = 64
RWKV_HEADS = BRANCH // RWKV_HEAD
RWKV_RANK = 64
RWKV_SHIFT_CH = 3 * BRANCH + 2 * RWKV_RANK
RWKV_GN_EPS = 64e-5

DIFF_HEADS = 4
DIFF_HEAD = BRANCH // (2 * DIFF_HEADS)

GQA_HEAD = 128
GQA_Q_HEADS = BRANCH // GQA_HEAD
GQA_KV_HEADS = 2

PROJ_SIZES = (
    BRANCH, SSM_CONV_CH, 2 * SSM_HEADS,
    RWKV_SHIFT_CH, BRANCH,
    BRANCH, BRANCH, BRANCH, BRANCH,
    BRANCH, GQA_KV_HEADS * GQA_HEAD, GQA_KV_HEADS * GQA_HEAD, BRANCH,
)
D_IN_PROJ = sum(PROJ_SIZES)

kernel_name = "hymba_style_bidir_ssd_rwkv7_diffattn_axialgqa"


def _split(u, sizes):
    idx, acc = [], 0
    for s in sizes[:-1]:
        acc += s
        idx.append(acc)
    return jnp.split(u, idx, axis=-1)


def _rms_norm(x, w, eps=NORM_EPS):
    xf = x.astype(jnp.float32)
    y = xf * lax.rsqrt(jnp.mean(xf * xf, axis=-1, keepdims=True) + eps)
    return (y * w.astype(jnp.float32)).astype(x.dtype)


def _rope(x, pos):
    d = x.shape[-1]
    half = d // 2
    inv = ROPE_THETA ** (-jnp.arange(half, dtype=jnp.float32) / half)
    ang = pos.astype(jnp.float32)[:, None] * inv[None, :]
    shape = (1, x.shape[1]) + (1,) * (x.ndim - 3) + (half,)
    cos, sin = jnp.cos(ang).reshape(shape), jnp.sin(ang).reshape(shape)
    xf = x.astype(jnp.float32)
    x1, x2 = xf[..., :half], xf[..., half:]
    return jnp.concatenate([x1 * cos - x2 * sin, x2 * cos + x1 * sin], axis=-1).astype(x.dtype)


def _axial_rope(x, row, col):
    half = x.shape[-1] // 2
    return jnp.concatenate([_rope(x[..., :half], row), _rope(x[..., half:], col)], axis=-1)


def _to_blocks(q):
    b, L = q.shape[:2]
    return jnp.moveaxis(q.reshape((b, L // Q_BLOCK, Q_BLOCK) + q.shape[2:]), 1, 0)


def _from_blocks(o):
    o = jnp.moveaxis(o, 0, 1)
    return o.reshape((o.shape[0], -1) + o.shape[3:])


def _centred_dwconv(u, w, bias):
    pad = D_CONV // 2
    out = lax.conv_general_dilated(
        u, w[:, None, :].astype(u.dtype), window_strides=(1,), padding=[(pad, pad)],
        dimension_numbers=("NWC", "WIO", "NWC"), feature_group_count=u.shape[-1])
    return out + bias.astype(u.dtype)


def _segsum(a):
    T = a.shape[-1]
    aa = jnp.broadcast_to(a[..., :, None], a.shape + (T,))
    aa = jnp.where(jnp.tril(jnp.ones((T, T), bool), -1), aa, 0.0)
    cs = jnp.cumsum(aa, axis=-2)
    return jnp.where(jnp.tril(jnp.ones((T, T), bool)), cs, -jnp.inf)


def _ssd(x, dt, A, bm, cm):
    b, L, h, p = x.shape
    nc = L // SSM_CHUNK
    chunk = lambda t: t.reshape((b, nc, SSM_CHUNK) + t.shape[2:])
    xdt = chunk(x * dt[..., None])
    bm, cm = chunk(bm), chunk(cm)
    a = jnp.moveaxis(chunk(dt * A), -1, 1)
    a_cum = jnp.cumsum(a, axis=-1)
    scores = jnp.einsum("bclhn,bcshn->bhcls", cm, bm) * jnp.exp(_segsum(a))
    y_diag = jnp.einsum("bhcls,bcshp->bclhp", scores, xdt)
    decay_states = jnp.moveaxis(jnp.exp(a_cum[..., -1:] - a_cum), 1, -1)
    states = jnp.einsum("bclhn,bclhp->bchpn", bm, xdt * decay_states[..., None])
    states = jnp.concatenate([jnp.zeros_like(states[:, :1]), states], axis=1)
    decay_chunk = jnp.exp(_segsum(jnp.pad(a_cum[..., -1], ((0, 0), (0, 0), (1, 0)))))
    states = jnp.einsum("bhzc,bchpn->bzhpn", decay_chunk, states)[:, :-1]
    state_decay_out = jnp.moveaxis(jnp.exp(a_cum), 1, -1)
    y_off = jnp.einsum("bclhn,bchpn->bclhp", cm, states) * state_decay_out[..., None]
    return (y_diag + y_off).reshape(b, L, h, p)


def _mamba_mixer(z, xbc, dt_raw, conv_w, conv_b, a_log, dt_bias, d_skip, norm_w):
    b, L, _ = z.shape
    xbc = jax.nn.silu(_centred_dwconv(xbc, conv_w, conv_b))
    xs, bm, cm = _split(xbc, (BRANCH, SSM_GROUPS * SSM_STATE, SSM_GROUPS * SSM_STATE))
    rep = SSM_HEADS // SSM_GROUPS
    xs = xs.reshape(b, L, SSM_HEADS, SSM_HEAD).astype(jnp.float32)
    bm = jnp.repeat(bm.reshape(b, L, SSM_GROUPS, SSM_STATE), rep, axis=2).astype(jnp.float32)
    cm = jnp.repeat(cm.reshape(b, L, SSM_GROUPS, SSM_STATE), rep, axis=2).astype(jnp.float32)
    dt = jax.nn.softplus(dt_raw.astype(jnp.float32).reshape(b, L, 2, SSM_HEADS) + dt_bias.astype(jnp.float32))
    A = -jnp.exp(a_log.astype(jnp.float32))
    flip = lambda t: jnp.flip(t, axis=1)
    y_f = _ssd(xs, dt[:, :, 0], A[0], bm, cm)
    y_b = flip(_ssd(flip(xs), flip(dt[:, :, 1]), A[1], flip(bm), flip(cm)))
    y = y_f + y_b + xs * d_skip.astype(jnp.float32)[:, None]
    y = y.reshape(b, L, BRANCH).astype(z.dtype)
    yg = (y * jax.nn.silu(z)).reshape(b, L, SSM_GROUPS, BRANCH // SSM_GROUPS)
    return _rms_norm(yg, norm_w.reshape(SSM_GROUPS, -1)).reshape(b, L, BRANCH)


def _token_shift_lerp(u, mu):
    prev = jnp.pad(u, ((0, 0), (1, 0), (0, 0)))[:, :-1]
    return u + (prev - u) * mu


def _wkv7_scan(r, w, k, v, a, bb):
    bsz, L, h, dk = r.shape
    tm = lambda t: jnp.moveaxis(t, 1, 0)

    def step(S, inp):
        r_t, w_t, k_t, v_t, a_t, b_t = inp
        sa = jnp.einsum("bhvk,bhk->bhv", S, a_t)
        S = S * w_t[:, :, None, :] + sa[..., None] * b_t[:, :, None, :] + v_t[..., None] * k_t[:, :, None, :]
        return S, jnp.einsum("bhvk,bhk->bhv", S, r_t)

    S0 = jnp.zeros((bsz, h, dk, dk), jnp.float32)
    _, y = lax.scan(step, S0, (tm(r), tm(w), tm(k), tm(v), tm(a), tm(bb)))
    return jnp.moveaxis(y, 0, 1)


def _rwkv_direction(u, mu, w0, w2, a0, a2, k_k, k_a, r_k):
    b, L, _ = u.shape
    u = _token_shift_lerp(u, mu)
    r, k, v, wd, ad = _split(u, (BRANCH, BRANCH, BRANCH, RWKV_RANK, RWKV_RANK))
    w = -jax.nn.softplus(-(w0 + jnp.tanh(wd) @ w2)) - 0.5
    decay = jnp.exp(-jnp.exp(w))
    a = jax.nn.sigmoid(a0 + ad @ a2)
    heads = lambda t: t.reshape(b, L, RWKV_HEADS, RWKV_HEAD)
    kk = heads(k * k_k)
    kk = kk / jnp.maximum(jnp.sqrt(jnp.sum(kk * kk, axis=-1, keepdims=True)), 1e-12)
    k = k * (1.0 + (a - 1.0) * k_a)
    r, k, v, decay, a = heads(r), heads(k), heads(v), heads(decay), heads(a)
    wkv = _wkv7_scan(r, decay, k, v, -kk, kk * a)
    bonus = jnp.sum(r * k * r_k, axis=-1, keepdims=True) * v
    return wkv, bonus


def _rwkv_mixer(u, mu, w0, w2, a0, a2, k_k, k_a, r_k, ln_w, ln_b):
    b, L, _ = u.shape
    f = lambda t: t.astype(jnp.float32)
    uf = f(u)
    shared = (f(a0), f(a2), f(k_k), f(k_a), f(r_k))
    flip = lambda t: jnp.flip(t, axis=1)
    wkv_f, bonus_f = _rwkv_direction(uf, f(mu[0]), f(w0[0]), f(w2[0]), *shared)
    wkv_b, bonus_b = _rwkv_direction(flip(uf), f(mu[1]), f(w0[1]), f(w2[1]), *shared)
    wkv = wkv_f + flip(wkv_b)
    m = jnp.mean(wkv, axis=-1, keepdims=True)
    var = jnp.mean(jnp.square(wkv - m), axis=-1, keepdims=True)
    gn = ((wkv - m) * lax.rsqrt(var + RWKV_GN_EPS)).reshape(b, L, BRANCH) * f(ln_w) + f(ln_b)
    y = gn + (bonus_f + flip(bonus_b)).reshape(b, L, BRANCH)
    return y.astype(u.dtype)


def _diff_mixer(q, k, v, lam_params, norm_w, pos, lambda_init):
    b, L, _ = q.shape
    q = _rope(q.reshape(b, L, DIFF_HEADS, 2, DIFF_HEAD), pos)
    k = _rope(k.reshape(b, L, DIFF_HEADS, 2, DIFF_HEAD), pos)
    v = v.reshape(b, L, DIFF_HEADS, 2 * DIFF_HEAD)
    lp = lam_params.astype(jnp.float32)
    lam = jnp.exp(jnp.sum(lp[0] * lp[1])) - jnp.exp(jnp.sum(lp[2] * lp[3])) + lambda_init
    scale = DIFF_HEAD ** -0.5

    def attend(qb):
        s = jnp.einsum("bqhcd,bkhcd->bhcqk", qb, k).astype(jnp.float32) * scale
        p = jax.nn.softmax(s, axis=-1)
        amap = (p[:, :, 0] - lam * p[:, :, 1]).astype(v.dtype)
        return jnp.einsum("bhqk,bkhe->bqhe", amap, v)

    o = _from_blocks(lax.map(attend, _to_blocks(q)))
    o = _rms_norm(o, norm_w) * (1.0 - lambda_init)
    return o.reshape(b, L, BRANCH)


def _gqa_mixer(q, k, v, q_norm_w, k_norm_w, row, col):
    b, L, _ = q.shape
    q = _axial_rope(_rms_norm(q.reshape(b, L, GQA_Q_HEADS, GQA_HEAD), q_norm_w), row, col)
    k = _axial_rope(_rms_norm(k.reshape(b, L, GQA_KV_HEADS, GQA_HEAD), k_norm_w), row, col)
    v = v.reshape(b, L, GQA_KV_HEADS, GQA_HEAD)
    q = q.reshape(b, L, GQA_KV_HEADS, GQA_Q_HEADS // GQA_KV_HEADS, GQA_HEAD)
    scale = GQA_HEAD ** -0.5

    def attend(qb):
        s = jnp.einsum("bqgrd,bkgd->bgrqk", qb, k).astype(jnp.float32) * scale
        p = jax.nn.softmax(s, axis=-1).astype(v.dtype)
        return jnp.einsum("bgrqk,bkgd->bqgrd", p, v)

    o = _from_blocks(lax.map(attend, _to_blocks(q)))
    return o.reshape(b, L, BRANCH)


def setup_inputs(seed: int = 0) -> dict:
    key = jax.random.key(seed)
    ks = iter(jax.random.split(key, 32))
    nrm = lambda shape, s: jax.random.normal(next(ks), shape, jnp.float32) * s
    uni = lambda shape, lo, hi: jax.random.uniform(next(ks), shape, jnp.float32, lo, hi)
    n = DEPTH
    x = nrm((BATCH, SEQ, D_MODEL), 1.0)
    pre_norm_w = 1.0 + nrm((n, D_MODEL), 0.02)
    post_norm_w = 1.0 + nrm((n, D_MODEL), 0.02)
    w_in = nrm((n, D_MODEL, D_IN_PROJ), D_MODEL ** -0.5)
    w_out = nrm((n, D_MIX, D_MODEL), D_MIX ** -0.5)
    conv_w = nrm((n, D_CONV, SSM_CONV_CH), D_CONV ** -0.5)
    conv_b = nrm((n, SSM_CONV_CH), 0.01)
    ssm_a_log = jnp.log(uni((n, 2, SSM_HEADS), 1.0, 16.0))
    dt0 = jnp.exp(uni((n, 2, SSM_HEADS), math.log(1e-3), math.log(1e-1)))
    ssm_dt_bias = dt0 + jnp.log(-jnp.expm1(-dt0))
    ssm_d = 1.0 + nrm((n, SSM_HEADS), 0.02)
    ssm_norm_w = 1.0 + nrm((n, BRANCH), 0.02)
    rwkv_mu = uni((n, 2, RWKV_SHIFT_CH), 0.0, 1.0)
    rwkv_w0 = uni((n, 2, BRANCH), -5.0, 0.0)
    rwkv_w2 = nrm((n, 2, RWKV_RANK, BRANCH), 0.1)
    rwkv_a0 = nrm((n, BRANCH), 0.1)
    rwkv_a2 = nrm((n, RWKV_RANK, BRANCH), 0.1)
    rwkv_k_k = 0.85 + nrm((n, BRANCH), 0.02)
    rwkv_k_a = 1.0 + nrm((n, BRANCH), 0.02)
    rwkv_r_k = nrm((n, RWKV_HEADS, RWKV_HEAD), 0.1)
    rwkv_ln_w = 1.0 + nrm((n, BRANCH), 0.02)
    rwkv_ln_b = nrm((n, BRANCH), 0.01)
    diff_lambda = nrm((n, 4, DIFF_HEAD), 0.1)
    diff_norm_w = 1.0 + nrm((n, 2 * DIFF_HEAD), 0.02)
    gqa_q_norm_w = 1.0 + nrm((n, GQA_HEAD), 0.02)
    gqa_k_norm_w = 1.0 + nrm((n, GQA_HEAD), 0.02)
    return {"x": x, "pre_norm_w": pre_norm_w, "post_norm_w": post_norm_w, "w_in": w_in, "w_out": w_out,
            "conv_w": conv_w, "conv_b": conv_b, "ssm_a_log": ssm_a_log, "ssm_dt_bias": ssm_dt_bias,
            "ssm_d": ssm_d, "ssm_norm_w": ssm_norm_w, "rwkv_mu": rwkv_mu, "rwkv_w0": rwkv_w0,
            "rwkv_w2": rwkv_w2, "rwkv_a0": rwkv_a0, "rwkv_a2": rwkv_a2, "rwkv_k_k": rwkv_k_k,
            "rwkv_k_a": rwkv_k_a, "rwkv_r_k": rwkv_r_k, "rwkv_ln_w": rwkv_ln_w, "rwkv_ln_b": rwkv_ln_b,
            "diff_lambda": diff_lambda, "diff_norm_w": diff_norm_w, "gqa_q_norm_w": gqa_q_norm_w,
            "gqa_k_norm_w": gqa_k_norm_w}


def reference(x, pre_norm_w, post_norm_w, w_in, w_out, conv_w, conv_b, ssm_a_log, ssm_dt_bias, ssm_d,
              ssm_norm_w, rwkv_mu, rwkv_w0, rwkv_w2, rwkv_a0, rwkv_a2, rwkv_k_k, rwkv_k_a, rwkv_r_k,
              rwkv_ln_w, rwkv_ln_b, diff_lambda, diff_norm_w, gqa_q_norm_w, gqa_k_norm_w):
    L = x.shape[1]
    rows = L // GRID_W
    pos = jnp.arange(L, dtype=jnp.int32)
    row = jnp.repeat(jnp.arange(rows, dtype=jnp.int32), GRID_W)
    col = jnp.tile(jnp.arange(GRID_W, dtype=jnp.int32), rows)
    for i in range(DEPTH):
        lambda_init = 0.8 - 0.6 * math.exp(-0.3 * i)
        h = _rms_norm(x, pre_norm_w[i])
        proj = jnp.einsum("bld,de->ble", h, w_in[i])
        (m_z, m_xbc, m_dt, r_u, r_g, d_q, d_k, d_v, d_g, g_q, g_k, g_v, g_g) = _split(proj, PROJ_SIZES)
        y_a = _mamba_mixer(m_z, m_xbc, m_dt, conv_w[i], conv_b[i], ssm_a_log[i], ssm_dt_bias[i],
                           ssm_d[i], ssm_norm_w[i])
        y_b = _rwkv_mixer(r_u, rwkv_mu[i], rwkv_w0[i], rwkv_w2[i], rwkv_a0[i], rwkv_a2[i], rwkv_k_k[i],
                          rwkv_k_a[i], rwkv_r_k[i], rwkv_ln_w[i], rwkv_ln_b[i]) * jax.nn.silu(r_g)
        y_c = _diff_mixer(d_q, d_k, d_v, diff_lambda[i], diff_norm_w[i], pos, lambda_init) * jax.nn.silu(d_g)
        y_d = _gqa_mixer(g_q, g_k, g_v, gqa_q_norm_w[i], gqa_k_norm_w[i], row, col) * jax.nn.silu(g_g)
        mix = jnp.einsum("ble,ed->bld", jnp.concatenate([y_a, y_b, y_c, y_d], axis=-1), w_out[i])
        x = x + _rms_norm(mix, post_norm_w[i])
    return x
```

```python
import functools
import math

import jax
import jax.numpy as jnp
from jax import lax
from jax.experimental import pallas as pl
from jax.experimental.pallas import tpu as pltpu

F32 = jnp.float32
BF16 = jnp.bfloat16

D_MODEL = 1024
GRID_W = 64
ROPE_THETA = 10000.0
NORM_EPS = 1e-6
BRANCH = 512
SSM_HEADS = 8
SSM_STATE = 128
D_CONV = 5
RWKV_RANK = 64
RWKV_GN_EPS = 64e-5
HEAD64 = 64
LANES = 128
SUBLANES = 8

C_XBC, C_Z = 0, 1024
C_RKV, C_RG = 1536, 3072
C_DQ, C_DK, C_DV, C_DG = 3584, 4096, 4608, 5120
C_GQ, C_GK, C_GV, C_GG = 5632, 6144, 6400, 6656
C_WA, C_DT = 7168, 7296
N_PROJ = 7424

SSD_CHUNK = 256
WKV_CHUNK = 64
FLASH_TQ = 512
FLASH_TK = 1024
VMEM_LIMIT = 48 * 1024 * 1024


def _cparams(sem):
    return pltpu.CompilerParams(dimension_semantics=sem, vmem_limit_bytes=VMEM_LIMIT)


def _dot(a, b):
    return jnp.dot(a, b, preferred_element_type=F32)


def _dot_nt(a, b):
    return lax.dot_general(a, b, (((1,), (1,)), ((), ())), preferred_element_type=F32)


def _split2(x):
    hi = x.astype(BF16)
    lo = (x - hi.astype(F32)).astype(BF16)
    return hi, lo


def _split3(x):
    hi = x.astype(BF16)
    r1 = x - hi.astype(F32)
    mid = r1.astype(BF16)
    lo = (r1 - mid.astype(F32)).astype(BF16)
    return hi, mid, lo


def _dot_sel_l(sel, x):
    hi, mid, lo = _split3(x)
    return _dot(sel, hi) + _dot(sel, mid) + _dot(sel, lo)


def _dot_sel_r(x, sel):
    hi, mid, lo = _split3(x)
    return _dot(hi, sel) + _dot(mid, sel) + _dot(lo, sel)


def _dot_hp(a, b):
    ah, al = _split2(a)
    bh, bl = _split2(b)
    return _dot(ah, bh) + _dot(ah, bl) + _dot(al, bh)


def _dot1(a, b):
    return _dot(a.astype(BF16), b.astype(BF16))


def _dot1_nt(a, b):
    return _dot_nt(a.astype(BF16), b.astype(BF16))


def _softplus(x):
    return jnp.maximum(x, 0.0) + jnp.log(1.0 + jnp.exp(-jnp.abs(x)))


def _silu(x):
    return x * jax.nn.sigmoid(x)


def _rms(x, w):
    return x * lax.rsqrt(jnp.mean(x * x, axis=-1, keepdims=True) + NORM_EPS) * w


def _inproj_body(x_ref, nw_ref, w_ref, o_ref):
    h = _rms(x_ref[...], nw_ref[...]).astype(BF16)
    o_ref[...] = _dot(h, w_ref[...])


def _in_proj(x2, nw, w):
    m = x2.shape[0]
    tm, tn = 256, N_PROJ // 2
    return pl.pallas_call(
        _inproj_body,
        grid=(N_PROJ // tn, m // tm),
        in_specs=[pl.BlockSpec((tm, D_MODEL), lambda j, i: (i, 0)),
                  pl.BlockSpec((1, D_MODEL), lambda j, i: (0, 0)),
                  pl.BlockSpec((D_MODEL, tn), lambda j, i: (0, j))],
        out_specs=pl.BlockSpec((tm, tn), lambda j, i: (i, j)),
        out_shape=jax.ShapeDtypeStruct((m, N_PROJ), F32),
        compiler_params=_cparams(("parallel", "parallel")),
        name="in_proj",
    )(x2, nw, w)


def _halo_specs(tl, width, col_block, nt, nrows):
    per = tl // SUBLANES
    last = nrows // SUBLANES - 1
    cur = pl.BlockSpec((tl, width), lambda b, i: (b * nt + i, col_block))
    prev = pl.BlockSpec((SUBLANES, width), lambda b, i: (jnp.maximum((b * nt + i) * per - 1, 0), col_block))
    nxt = pl.BlockSpec((SUBLANES, width), lambda b, i: (jnp.minimum((b * nt + i + 1) * per, last), col_block))
    return cur, prev, nxt


def _fill_halo(scr, cur_ref, prev_ref, next_ref):
    i = pl.program_id(1)
    tl = cur_ref.shape[0]
    scr[0:SUBLANES, :] = jnp.where(i > 0, prev_ref[...], 0.0)
    scr[SUBLANES:SUBLANES + tl, :] = cur_ref[...]
    scr[SUBLANES + tl:2 * SUBLANES + tl, :] = jnp.where(i < pl.num_programs(1) - 1, next_ref[...], 0.0)


def _conv_body(cur_ref, prev_ref, next_ref, w_ref, b_ref, o_ref, scr):
    _fill_halo(scr, cur_ref, prev_ref, next_ref)
    tl = cur_ref.shape[0]
    acc = b_ref[...] + w_ref[0:1, :] * scr[pl.ds(SUBLANES - 2, tl), :]
    for j in range(1, D_CONV):
        acc = acc + w_ref[j:j + 1, :] * scr[pl.ds(SUBLANES - 2 + j, tl), :]
    o_ref[...] = _silu(acc)


def _conv(proj, conv_w8, conv_b, bsz, seq):
    tl = min(512, seq)
    nt = seq // tl
    width = 1024
    cur, prev, nxt = _halo_specs(tl, width, C_XBC // width, nt, bsz * seq)
    return pl.pallas_call(
        _conv_body,
        grid=(bsz, nt),
        in_specs=[cur, prev, nxt,
                  pl.BlockSpec((SUBLANES, width), lambda b, i: (0, 0)),
                  pl.BlockSpec((1, width), lambda b, i: (0, 0))],
        out_specs=pl.BlockSpec((tl, width), lambda b, i: (b * nt + i, 0)),
        out_shape=jax.ShapeDtypeStruct((bsz * seq, width), F32),
        scratch_shapes=[pltpu.VMEM((tl + 2 * SUBLANES, width), F32)],
        compiler_params=_cparams(("parallel", "parallel")),
        name="ssm_conv",
    )(proj, proj, proj, conv_w8, conv_b)


def _ssd_body(reverse, *refs):
    if reverse:
        (xs_ref, bc_ref, dt_ref, dtb_ref, alog_ref, e_ref, yf_ref, z_ref, dsk_ref, nw_ref, o_ref, st_ref) = refs
    else:
        (xs_ref, bc_ref, dt_ref, dtb_ref, alog_ref, e_ref, o_ref, st_ref) = refs
    c = pl.program_id(1)

    @pl.when(c == 0)
    def _():
        st_ref[...] = jnp.zeros_like(st_ref)

    t = xs_ref.shape[0]
    row = lax.broadcasted_iota(jnp.int32, (t, t), 0)
    col = lax.broadcasted_iota(jnp.int32, (t, t), 1)
    tri = (col >= row) if reverse else (col <= row)
    tri_bf = jnp.where(tri, 1.0, 0.0).astype(BF16)
    dt_all = _softplus(dt_ref[...] + dtb_ref[...])
    a_all = dt_all * (-jnp.exp(alog_ref[...]))
    cum = _dot_sel_l(tri_bf, a_all)
    sel = e_ref[...]
    dt_e = _dot_sel_r(dt_all, sel)
    cum_e = _dot_sel_r(cum, sel)
    tot_e = cum_e[0:1, :] if reverse else cum_e[t - 1:t, :]
    cum_t = cum.T
    xs = xs_ref[...]
    xdt = xs * dt_e
    bc = bc_ref[...]
    lane = lax.broadcasted_iota(jnp.int32, (t, LANES), 1)
    first = lane < HEAD64
    ofs = SSM_HEADS if reverse else 0
    ys = []
    for g in range(2):
        bm = bc[:, g * SSM_STATE:(g + 1) * SSM_STATE]
        cm = bc[:, 256 + g * SSM_STATE:256 + (g + 1) * SSM_STATE].astype(BF16)
        cb = _dot_nt(cm, bm.astype(BF16))
        bm_t = bm.T.astype(BF16)
        for pr in range(2):
            p = g * 2 + pr
            sl = slice(p * LANES, (p + 1) * LANES)
            xdt_p = xdt[:, sl]
            xdt_bf = xdt_p.astype(BF16)
            halves = []
            for hh in range(2):
                ci = ofs + 2 * p + hh
                dec = jnp.exp(jnp.where(tri, cum[:, ci:ci + 1] - cum_t[ci:ci + 1, :], -jnp.inf))
                halves.append(_dot((cb * dec).astype(BF16), xdt_bf))
            y_diag = jnp.where(first, halves[0], halves[1])
            state = st_ref[p]
            cum_p = cum_e[:, sl]
            y_off = _dot(cm, state.astype(BF16)) * jnp.exp(cum_p)
            ys.append(y_diag + y_off)
            tot_p = tot_e[:, sl]
            loc = _dot(bm_t, (xdt_p * jnp.exp(tot_p - cum_p)).astype(BF16))
            st_ref[p] = state * jnp.exp(tot_p) + loc
    y = jnp.concatenate(ys, axis=1)
    if not reverse:
        o_ref[...] = y
        return
    y = y + yf_ref[...] + xs * dsk_ref[...]
    yg = y * _silu(z_ref[...])
    nw = nw_ref[...]
    half = BRANCH // 2
    o_ref[...] = jnp.concatenate(
        [_rms(yg[:, g * half:(g + 1) * half], nw[:, g * half:(g + 1) * half]) for g in range(2)], axis=1)


def _ssd(reverse, xbc, proj, dt_bias, a_log, sel, extra, bsz, seq):
    t = min(SSD_CHUNK, seq)
    nc = seq // t
    if reverse:
        rmap = lambda b, c: (b * nc + nc - 1 - c)
    else:
        rmap = lambda b, c: (b * nc + c)
    in_specs = [pl.BlockSpec((t, BRANCH), lambda b, c: (rmap(b, c), 0)),
                pl.BlockSpec((t, BRANCH), lambda b, c: (rmap(b, c), 1)),
                pl.BlockSpec((t, LANES), lambda b, c: (rmap(b, c), C_DT // LANES)),
                pl.BlockSpec((1, LANES), lambda b, c: (0, 0)),
                pl.BlockSpec((1, LANES), lambda b, c: (0, 0)),
                pl.BlockSpec((LANES, BRANCH), lambda b, c: (0, 0))]
    args = [xbc, xbc, proj, dt_bias, a_log, sel]
    if reverse:
        y_f, d_skip, norm_w = extra
        in_specs += [pl.BlockSpec((t, BRANCH), lambda b, c: (rmap(b, c), 0)),
                     pl.BlockSpec((t, BRANCH), lambda b, c: (rmap(b, c), C_Z // BRANCH)),
                     pl.BlockSpec((1, BRANCH), lambda b, c: (0, 0)),
                     pl.BlockSpec((1, BRANCH), lambda b, c: (0, 0))]
        args += [y_f, proj, d_skip, norm_w]
    return pl.pallas_call(
        functools.partial(_ssd_body, reverse),
        grid=(bsz, nc),
        in_specs=in_specs,
        out_specs=pl.BlockSpec((t, BRANCH), lambda b, c: (rmap(b, c), 0)),
        out_shape=jax.ShapeDtypeStruct((bsz * seq, BRANCH), F32),
        scratch_shapes=[pltpu.VMEM((4, SSM_STATE, LANES), F32)],
        compiler_params=_cparams(("parallel", "arbitrary")),
        name="ssd_bwd" if reverse else "ssd_fwd",
    )(*args)


def _rwkv_prep_body(cur_ref, prev_ref, next_ref, wcur_ref, wprev_ref, wnext_ref, mu_ref, muw_ref, w0_ref, w2_ref,
                    a0_ref, a2_ref, kk_ref, ka_ref, rk_ref, blk_ref, *outs):
    scr, wscr = outs[-2:]
    bonus_ref = outs[-3]
    o_refs = outs[:-3]
    _fill_halo(scr, cur_ref, prev_ref, next_ref)
    _fill_halo(wscr, wcur_ref, wprev_ref, wnext_ref)
    tl = cur_ref.shape[0]
    u0 = cur_ref[...]
    wa0 = wcur_ref[...]
    blk = blk_ref[...]

    def head_sum(x):
        return jnp.concatenate(
            [_dot_sel_r(x[:, j * LANES:(j + 1) * LANES], blk) for j in range(BRANCH // LANES)], axis=1)

    bonus = None
    for d in range(2):
        off = SUBLANES - 1 if d == 0 else SUBLANES + 1
        u = u0 + (scr[pl.ds(off, tl), :] - u0) * mu_ref[d:d + 1, :]
        wa = wa0 + (wscr[pl.ds(off, tl), :] - wa0) * muw_ref[d:d + 1, :]
        r, k, v = u[:, 0:BRANCH], u[:, BRANCH:2 * BRANCH], u[:, 2 * BRANCH:3 * BRANCH]
        wlin = w0_ref[d:d + 1, :] + _dot_hp(jnp.tanh(wa), w2_ref[d])
        w = -_softplus(-wlin) - 0.5
        lw = -jnp.exp(w)
        eta = jax.nn.sigmoid(a0_ref[...] + _dot_hp(wa, a2_ref[...]))
        kk = k * kk_ref[...]
        kk = kk / jnp.maximum(jnp.sqrt(head_sum(kk * kk)), 1e-12)
        k2 = k * (1.0 + (eta - 1.0) * ka_ref[...])
        o_refs[6 * d + 0][...] = r
        o_refs[6 * d + 1][...] = k2
        o_refs[6 * d + 2][...] = v
        o_refs[6 * d + 3][...] = lw
        o_refs[6 * d + 4][...] = -kk
        o_refs[6 * d + 5][...] = kk * eta
        bd = head_sum(r * k2 * rk_ref[...]) * v
        bonus = bd if bonus is None else bonus + bd
    bonus_ref[...] = bonus


def _rwkv_prep(proj, prm, bsz, seq):
    tl = min(256, seq)
    nt = seq // tl
    m = bsz * seq
    wide = 3 * BRANCH
    cur, prev, nxt = _halo_specs(tl, wide, C_RKV // wide, nt, m)
    wcur, wprev, wnxt = _halo_specs(tl, LANES, C_WA // LANES, nt, m)
    full = lambda shape: pl.BlockSpec(shape, lambda b, i: (0,) * len(shape))
    out_spec = pl.BlockSpec((tl, BRANCH), lambda b, i: (b * nt + i, 0))
    outs = pl.pallas_call(
        _rwkv_prep_body,
        grid=(bsz, nt),
        in_specs=[cur, prev, nxt, wcur, wprev, wnxt,
                  full((2, wide)), full((2, LANES)), full((2, BRANCH)), full((2, LANES, BRANCH)),
                  full((1, BRANCH)), full((LANES, BRANCH)), full((1, BRANCH)), full((1, BRANCH)),
                  full((1, BRANCH)), full((LANES, LANES))],
        out_specs=[out_spec] * 13,
        out_shape=[jax.ShapeDtypeStruct((m, BRANCH), F32)] * 13,
        scratch_shapes=[pltpu.VMEM((tl + 2 * SUBLANES, wide), F32), pltpu.VMEM((tl + 2 * SUBLANES, LANES), F32)],
        compiler_params=_cparams(("parallel", "parallel")),
        name="rwkv_prep",
    )(proj, proj, proj, proj, proj, proj, prm["mu_rkv"], prm["mu_wa"], prm["w0"], prm["w2"], prm["a0"], prm["a2"],
      prm["k_k"], prm["k_a"], prm["r_k"], prm["blk"])
    return outs[0:6], outs[6:12], outs[12]


def _wkv_body(reverse, *refs):
    if reverse:
        (r_ref, k_ref, v_ref, lw_ref, a_ref, b_ref, yf_ref, bonus_ref, g_ref, lnw_ref, lnb_ref, blk_ref,
         o_ref, s_ref) = refs
    else:
        (r_ref, k_ref, v_ref, lw_ref, a_ref, b_ref, o_ref, s_ref) = refs
    ci = pl.program_id(2)

    @pl.when(ci == 0)
    def _():
        s_ref[...] = jnp.zeros_like(s_ref)

    c = r_ref.shape[0]
    c2 = 2 * c
    r, k, v, lw, a, b = r_ref[...], k_ref[...], v_ref[...], lw_ref[...], a_ref[...], b_ref[...]
    row = lax.broadcasted_iota(jnp.int32, (c, c), 0)
    col = lax.broadcasted_iota(jnp.int32, (c, c), 1)
    tri_bf = jnp.where((col >= row) if reverse else (col <= row), 1.0, 0.0).astype(BF16)
    cum = _dot_sel_l(tri_bf, lw)
    cum_prev = cum - lw
    tot = cum[0:1, :] if reverse else cum[c - 1:c, :]
    mid = cum[c // 2:c // 2 + 1, :]
    e_out = jnp.exp(cum - mid)
    e_in = jnp.exp(mid - cum)
    e_end = jnp.exp(tot - cum)
    lane = lax.broadcasted_iota(jnp.int32, (c, LANES), 1)
    first = lane < HEAD64

    def stack(x):
        return jnp.concatenate([jnp.where(first, x, 0.0), jnp.where(first, 0.0, x)], axis=0)

    am_s = stack(a * jnp.exp(cum_prev - mid))
    bm_s = stack(b * e_in)
    km_s = stack(k * e_in)
    rm_s = stack(r * e_out)
    a0_s = stack(a * jnp.exp(cum_prev))
    r0_s = stack(r * jnp.exp(cum))
    v_s = stack(v)
    bend_t = stack(b * e_end).T
    kend_t = stack(k * e_end).T
    row2 = lax.broadcasted_iota(jnp.int32, (c2, c2), 0)
    col2 = lax.broadcasted_iota(jnp.int32, (c2, c2), 1)
    same = jnp.where(row2 >= c, 1, 0) == jnp.where(col2 >= c, 1, 0)
    if reverse:
        strict, incl = same & (col2 > row2), same & (col2 >= row2)
    else:
        strict, incl = same & (col2 < row2), same & (col2 <= row2)
    l_ab = jnp.where(strict, _dot1_nt(am_s, bm_s), 0.0)
    l_ak = jnp.where(strict, _dot1_nt(am_s, km_s), 0.0)
    a_rb = jnp.where(incl, _dot1_nt(rm_s, bm_s), 0.0)
    a_rk = jnp.where(incl, _dot1_nt(rm_s, km_s), 0.0)
    eye2 = jnp.where(row2 == col2, 1.0, 0.0)
    tinv = eye2 + l_ab
    lp = l_ab
    steps = max(1, int(math.ceil(math.log2(c))) - 1)
    for _ in range(steps):
        lp = _dot_hp(lp, lp)
        tinv = tinv + _dot_hp(lp, tinv)
    a_hat = _dot_hp(tinv, a0_s)
    u0 = _dot_hp(tinv, _dot1(l_ak, v_s))
    r_hat = r0_s + _dot1(a_rb, a_hat)
    y0 = _dot1(a_rb, u0) + _dot1(a_rk, v_s)
    rowk = lax.broadcasted_iota(jnp.int32, (LANES, LANES), 0)
    colk = lax.broadcasted_iota(jnp.int32, (LANES, LANES), 1)
    gmat = jnp.where(rowk == colk, jnp.exp(tot), 0.0) + _dot1(bend_t, a_hat)
    hmat = _dot1(bend_t, u0) + _dot1(kend_t, v_s)
    state = s_ref[...]
    y_s = _dot_hp(r_hat, state) + y0
    s_ref[...] = _dot_hp(gmat, state) + hmat
    y = y_s[0:c, :] + y_s[c:c2, :]
    if not reverse:
        o_ref[...] = y
        return
    wkv = y + yf_ref[...]
    blk = blk_ref[...]
    inv_n = 1.0 / HEAD64
    mean = _dot_sel_r(wkv, blk) * inv_n
    cen = wkv - mean
    var = _dot_sel_r(cen * cen, blk) * inv_n
    gn = cen * lax.rsqrt(var + RWKV_GN_EPS) * lnw_ref[...] + lnb_ref[...]
    o_ref[...] = (gn + bonus_ref[...]) * _silu(g_ref[...])


def _wkv(reverse, ops, extra, bsz, seq):
    c = min(WKV_CHUNK, seq)
    nc = seq // c
    npair = BRANCH // LANES
    if reverse:
        rmap = lambda b, ci: (b * nc + nc - 1 - ci)
    else:
        rmap = lambda b, ci: (b * nc + ci)
    tile = pl.BlockSpec((c, LANES), lambda b, p, ci: (rmap(b, ci), p))
    in_specs = [tile] * 6
    args = list(ops)
    if reverse:
        y_f, bonus, proj, ln_w, ln_b, blk = extra
        in_specs += [tile, tile,
                     pl.BlockSpec((c, LANES), lambda b, p, ci: (rmap(b, ci), C_RG // LANES + p)),
                     pl.BlockSpec((1, LANES), lambda b, p, ci: (0, p)),
                     pl.BlockSpec((1, LANES), lambda b, p, ci: (0, p)),
                     pl.BlockSpec((LANES, LANES), lambda b, p, ci: (0, 0))]
        args += [y_f, bonus, proj, ln_w, ln_b, blk]
    return pl.pallas_call(
        functools.partial(_wkv_body, reverse),
        grid=(bsz, npair, nc),
        in_specs=in_specs,
        out_specs=tile,
        out_shape=jax.ShapeDtypeStruct((bsz * seq, BRANCH), F32),
        scratch_shapes=[pltpu.VMEM((LANES, LANES), F32)],
        compiler_params=_cparams(("parallel", "parallel", "arbitrary")),
        name="wkv_bwd" if reverse else "wkv_fwd",
    )(*args)


def _rope(x, cos, sin_signed):
    width = x.shape[1]
    reps = width // LANES
    lane = lax.broadcasted_iota(jnp.int32, x.shape, 1)
    first_half = (lane & (HEAD64 - 1)) < (HEAD64 // 2)
    partner = jnp.where(first_half, pltpu.roll(x, width - HEAD64 // 2, 1), pltpu.roll(x, HEAD64 // 2, 1))
    cos_w = jnp.concatenate([cos] * reps, axis=1) if reps > 1 else cos
    sin_w = jnp.concatenate([sin_signed] * reps, axis=1) if reps > 1 else sin_signed
    return x * cos_w + partner * sin_w


def _attn_prep_body(dq_ref, dk_ref, dv_ref, gq_ref, gk_ref, gv_ref, cd_ref, sd_ref, cg_ref, sg_ref, qw_ref, kw_ref,
                    q1_ref, q2_ref, dko_ref, dvo_ref, gqo_ref, gko_ref, gvo_ref):
    cd, sd, cg, sg = cd_ref[...], sd_ref[...], cg_ref[...], sg_ref[...]
    q = _rope(dq_ref[...], cd, sd) * (HEAD64 ** -0.5)
    lane = lax.broadcasted_iota(jnp.int32, q.shape, 1)
    comp0 = (lane & (LANES - 1)) < HEAD64
    q1_ref[...] = jnp.where(comp0, q, 0.0).astype(BF16)
    q2_ref[...] = jnp.where(comp0, 0.0, q).astype(BF16)
    dko_ref[...] = _rope(dk_ref[...], cd, sd).astype(BF16)
    dvo_ref[...] = dv_ref[...].astype(BF16)

    def norm_heads(x, w):
        return jnp.concatenate(
            [_rms(x[:, j * LANES:(j + 1) * LANES], w) for j in range(x.shape[1] // LANES)], axis=1)

    gq = _rope(norm_heads(gq_ref[...], qw_ref[...]), cg, sg) * (LANES ** -0.5)
    gqo_ref[...] = gq.astype(BF16)
    gko_ref[...] = _rope(norm_heads(gk_ref[...], kw_ref[...]), cg, sg).astype(BF16)
    gvo_ref[...] = gv_ref[...].astype(BF16)


def _attn_prep(proj, tabs, qw, kw, bsz, seq):
    tl = min(256, seq)
    nt = seq // tl
    m = bsz * seq
    colspec = lambda width, off: pl.BlockSpec((tl, width), lambda b, i: (b * nt + i, off // width))
    tab = pl.BlockSpec((tl, LANES), lambda b, i: (i, 0))
    wspec = pl.BlockSpec((1, LANES), lambda b, i: (0, 0))
    o512 = pl.BlockSpec((tl, BRANCH), lambda b, i: (b * nt + i, 0))
    o256 = pl.BlockSpec((tl, 256), lambda b, i: (b * nt + i, 0))
    s512 = jax.ShapeDtypeStruct((m, BRANCH), BF16)
    s256 = jax.ShapeDtypeStruct((m, 256), BF16)
    return pl.pallas_call(
        _attn_prep_body,
        grid=(bsz, nt),
        in_specs=[colspec(BRANCH, C_DQ), colspec(BRANCH, C_DK), colspec(BRANCH, C_DV), colspec(BRANCH, C_GQ),
                  colspec(256, C_GK), colspec(256, C_GV), tab, tab, tab, tab, wspec, wspec],
        out_specs=[o512, o512, o512, o512, o512, o256, o256],
        out_shape=[s512, s512, s512, s512, s512, s256, s256],
        compiler_params=_cparams(("parallel", "parallel")),
        name="attn_prep",
    )(proj, proj, proj, proj, proj, proj, *tabs, qw, kw)


def _flash_body(diff, scale_out, *refs):
    if diff:
        qa_ref, qb_ref, k_ref, v_ref, lam_ref, nw_ref, o_ref, m_ref, l_ref, acc_ref = refs
    else:
        qa_ref, qb_ref, k_ref, v_ref, o_ref, m_ref, l_ref, acc_ref = refs
    kv = pl.program_id(3)
    tq = qa_ref.shape[0]

    @pl.when(kv == 0)
    def _():
        m_ref[...] = jnp.full_like(m_ref, -jnp.inf)
        l_ref[...] = jnp.zeros_like(l_ref)
        acc_ref[...] = jnp.zeros_like(acc_ref)

    q = jnp.concatenate([qa_ref[...], qb_ref[...]], axis=0)
    s = _dot_nt(q, k_ref[...])
    m_prev = m_ref[...]
    m_new = jnp.maximum(m_prev, jnp.max(s, axis=1, keepdims=True))
    alpha = jnp.exp(m_prev - m_new)
    p = jnp.exp(s - m_new)
    l_ref[...] = alpha * l_ref[...] + jnp.sum(p, axis=1, keepdims=True)
    acc_ref[...] = alpha * acc_ref[...] + _dot(p.astype(BF16), v_ref[...])
    m_ref[...] = m_new

    @pl.when(kv == pl.num_programs(3) - 1)
    def _():
        o = acc_ref[...] / l_ref[...]
        oa, ob = o[0:tq, :], o[tq:2 * tq, :]
        if diff:
            lp = lam_ref[...]
            lam = (jnp.exp(jnp.sum(lp[0:1, :] * lp[1:2, :], axis=1, keepdims=True))
                   - jnp.exp(jnp.sum(lp[2:3, :] * lp[3:4, :], axis=1, keepdims=True)) + (1.0 - scale_out))
            o_ref[...] = _rms(oa - lam * ob, nw_ref[...]) * scale_out
        else:
            o_ref[...] = jnp.concatenate([oa, ob], axis=1)


def _flash(diff, qa, qb, k, v, extra, lambda_init, bsz, seq):
    tq = min(FLASH_TQ, seq)
    tk = min(FLASH_TK, seq)
    nq, nk = seq // tq, seq // tk
    groups = k.shape[1] // LANES
    qa_spec = pl.BlockSpec((tq, LANES), lambda b, g, i, j: (b * nq + i, g if diff else 2 * g))
    qb_spec = pl.BlockSpec((tq, LANES), lambda b, g, i, j: (b * nq + i, g if diff else 2 * g + 1))
    kv_spec = pl.BlockSpec((tk, LANES), lambda b, g, i, j: (b * nk + j, g))
    in_specs = [qa_spec, qb_spec, kv_spec, kv_spec]
    args = [qa, qb, k, v]
    if diff:
        in_specs += [pl.BlockSpec((4, HEAD64), lambda b, g, i, j: (0, 0)),
                     pl.BlockSpec((1, LANES), lambda b, g, i, j: (0, 0))]
        args += list(extra)
        out_spec = pl.BlockSpec((tq, LANES), lambda b, g, i, j: (b * nq + i, g))
    else:
        out_spec = pl.BlockSpec((tq, 2 * LANES), lambda b, g, i, j: (b * nq + i, g))
    return pl.pallas_call(
        functools.partial(_flash_body, diff, 1.0 - lambda_init),
        grid=(bsz, groups, nq, nk),
        in_specs=in_specs,
        out_specs=out_spec,
        out_shape=jax.ShapeDtypeStruct((bsz * seq, BRANCH), F32),
        scratch_shapes=[pltpu.VMEM((2 * tq, 1), F32), pltpu.VMEM((2 * tq, 1), F32),
                        pltpu.VMEM((2 * tq, LANES), F32)],
        compiler_params=_cparams(("parallel", "parallel", "parallel", "arbitrary")),
        name="diff_attn" if diff else "gqa_attn",
    )(*args)


def _outproj_body(x_ref, ya_ref, yb_ref, oc_ref, dg_ref, od_ref, gg_ref, w_ref, pw_ref, o_ref):
    yc = oc_ref[...] * _silu(dg_ref[...])
    yd = od_ref[...] * _silu(gg_ref[...])
    mix = (_dot(ya_ref[...].astype(BF16), w_ref[0]) + _dot(yb_ref[...].astype(BF16), w_ref[1])
           + _dot(yc.astype(BF16), w_ref[2]) + _dot(yd.astype(BF16), w_ref[3]))
    o_ref[...] = x_ref[...] + _rms(mix, pw_ref[...])


def _out_proj(x2, y_a, y_b, o_c, o_d, proj, w4, pw):
    m = x2.shape[0]
    tm = 256
    rows = lambda width, cb: pl.BlockSpec((tm, width), lambda i: (i, cb))
    return pl.pallas_call(
        _outproj_body,
        grid=(m // tm,),
        in_specs=[rows(D_MODEL, 0), rows(BRANCH, 0), rows(BRANCH, 0), rows(BRANCH, 0), rows(BRANCH, C_DG // BRANCH),
                  rows(BRANCH, 0), rows(BRANCH, C_GG // BRANCH),
                  pl.BlockSpec((4, BRANCH, D_MODEL), lambda i: (0, 0, 0)),
                  pl.BlockSpec((1, D_MODEL), lambda i: (0, 0))],
        out_specs=rows(D_MODEL, 0),
        out_shape=jax.ShapeDtypeStruct((m, D_MODEL), F32),
        compiler_params=_cparams(("parallel",)),
        name="out_proj",
    )(x2, y_a, y_b, o_c, proj, o_d, proj, w4, pw)


def _reorder_w_in(w):
    parts = [w[:, 512:1536], w[:, 0:512], w[:, 1552:3088], w[:, 3216:3728], w[:, 3728:5776], w[:, 5776:7312],
             w[:, 3088:3216], w[:, 1536:1552], jnp.zeros((w.shape[0], N_PROJ - 7312), w.dtype)]
    return jnp.concatenate(parts, axis=1).astype(BF16)


def _rope_tables(seq):
    half = HEAD64 // 2
    inv = ROPE_THETA ** (-jnp.arange(half, dtype=F32) / half)
    t = jnp.arange(seq, dtype=jnp.int32)

    def tab(pos):
        ang = pos.astype(F32)[:, None] * inv[None, :]
        cos, sin = jnp.cos(ang), jnp.sin(ang)
        return jnp.concatenate([cos, cos], axis=1), jnp.concatenate([-sin, sin], axis=1)

    cp, sp = tab(t)
    cr, sr = tab(t // GRID_W)
    cc, sc = tab(t % GRID_W)
    return (jnp.concatenate([cp, cp], axis=1), jnp.concatenate([sp, sp], axis=1),
            jnp.concatenate([cr, cc], axis=1), jnp.concatenate([sr, sc], axis=1))


def _pad_lanes(v, width=LANES):
    v = v.reshape(1, -1)
    return jnp.pad(v, ((0, 0), (0, width - v.shape[1])))


def kernel(x, pre_norm_w, post_norm_w, w_in, w_out, conv_w, conv_b, ssm_a_log, ssm_dt_bias, ssm_d, ssm_norm_w,
           rwkv_mu, rwkv_w0, rwkv_w2, rwkv_a0, rwkv_a2, rwkv_k_k, rwkv_k_a, rwkv_r_k, rwkv_ln_w, rwkv_ln_b,
           diff_lambda, diff_norm_w, gqa_q_norm_w, gqa_k_norm_w):
    bsz, seq, _ = x.shape
    depth = w_in.shape[0]
    m = bsz * seq
    x2 = x.reshape(m, D_MODEL)
    tabs = _rope_tables(seq)
    lane_head = jnp.arange(LANES) // HEAD64
    blk = (lane_head[:, None] == lane_head[None, :]).astype(BF16)
    head_of_lane = jnp.arange(BRANCH) // HEAD64
    sels = [(jnp.arange(LANES)[:, None] == (d * SSM_HEADS + head_of_lane)[None, :]).astype(BF16) for d in range(2)]
    zpad = jnp.zeros((RWKV_RANK, BRANCH), F32)
    for i in range(depth):
        lambda_init = 0.8 - 0.6 * math.exp(-0.3 * i)
        proj = _in_proj(x2, pre_norm_w[i].reshape(1, -1), _reorder_w_in(w_in[i]))
        xbc = _conv(proj, jnp.pad(conv_w[i], ((0, SUBLANES - D_CONV), (0, 0))), conv_b[i].reshape(1, -1), bsz, seq)
        dt_bias = _pad_lanes(ssm_dt_bias[i])
        a_log = _pad_lanes(ssm_a_log[i])
        y_f = _ssd(False, xbc, proj, dt_bias, a_log, sels[0], None, bsz, seq)
        d_skip = jnp.repeat(ssm_d[i], HEAD64).reshape(1, -1)
        y_a = _ssd(True, xbc, proj, dt_bias, a_log, sels[1], (y_f, d_skip, ssm_norm_w[i].reshape(1, -1)), bsz, seq)
        prm = {
            "mu_rkv": rwkv_mu[i][:, :3 * BRANCH], "mu_wa": rwkv_mu[i][:, 3 * BRANCH:], "w0": rwkv_w0[i],
            "w2": jnp.concatenate([rwkv_w2[i], jnp.zeros((2, RWKV_RANK, BRANCH), F32)], axis=1),
            "a0": rwkv_a0[i].reshape(1, -1), "a2": jnp.concatenate([zpad, rwkv_a2[i]], axis=0),
            "k_k": rwkv_k_k[i].reshape(1, -1), "k_a": rwkv_k_a[i].reshape(1, -1),
            "r_k": rwkv_r_k[i].reshape(1, -1), "blk": blk,
        }
        ops_f, ops_b, bonus = _rwkv_prep(proj, prm, bsz, seq)
        wkv_f = _wkv(False, ops_f, None, bsz, seq)
        y_b = _wkv(True, ops_b, (wkv_f, bonus, proj, rwkv_ln_w[i].reshape(1, -1), rwkv_ln_b[i].reshape(1, -1), blk),
                   bsz, seq)
        q1, q2, dk, dv, gq, gk, gv = _attn_prep(proj, tabs, gqa_q_norm_w[i].reshape(1, -1),
                                                gqa_k_norm_w[i].reshape(1, -1), bsz, seq)
        o_c = _flash(True, q1, q2, dk, dv, (diff_lambda[i], diff_norm_w[i].reshape(1, -1)), lambda_init, bsz, seq)
        o_d = _flash(False, gq, gq, gk, gv, None, lambda_init, bsz, seq)
        x2 = _out_proj(x2, y_a, y_b, o_c, o_d, proj, w_out[i].reshape(4, BRANCH, D_MODEL).astype(BF16),
                       post_norm_w[i].reshape(1, -1))
    return x2.reshape(bsz, seq, D_MODEL)
```

```python
import functools
import math

import jax
import jax.numpy as jnp
from jax import lax
from jax.experimental import pallas as pl
from jax.experimental.pallas import tpu as pltpu

F32 = jnp.float32
BF16 = jnp.bfloat16

D_MODEL = 1024
GRID_W = 64
ROPE_THETA = 10000.0
NORM_EPS = 1e-6
BRANCH = 512
SSM_HEADS = 8
SSM_STATE = 128
D_CONV = 5
RWKV_RANK = 64
RWKV_GN_EPS = 64e-5
HEAD64 = 64
LANES = 128
SUBLANES = 8

C_XBC, C_Z = 0, 1024
C_RKV, C_RG = 1536, 3072
C_DQ, C_DK, C_DV, C_DG = 3584, 4096, 4608, 5120
C_GQ, C_GK, C_GV, C_GG = 5632, 6144, 6400, 6656
C_WA, C_DT = 7168, 7296
N_PROJ = 7424

SSD_CHUNK = 256
WKV_CHUNK = 64
FLASH_TQ = 512
FLASH_CK = 512
FLASH_UNROLL = 1
VT_ROWS = LANES + 16
LOG2E = 1.4426950408889634
VMEM_LIMIT = 48 * 1024 * 1024


def _cparams(sem):
    return pltpu.CompilerParams(dimension_semantics=sem, vmem_limit_bytes=VMEM_LIMIT)


def _dot(a, b):
    return jnp.dot(a, b, preferred_element_type=F32)


def _dot_nt(a, b):
    return lax.dot_general(a, b, (((1,), (1,)), ((), ())), preferred_element_type=F32)


def _split2(x):
    hi = x.astype(BF16)
    lo = (x - hi.astype(F32)).astype(BF16)
    return hi, lo


def _split3(x):
    hi = x.astype(BF16)
    r1 = x - hi.astype(F32)
    mid = r1.astype(BF16)
    lo = (r1 - mid.astype(F32)).astype(BF16)
    return hi, mid, lo


def _dot_sel_l(sel, x):
    hi, mid, lo = _split3(x)
    return _dot(sel, hi) + _dot(sel, mid) + _dot(sel, lo)


def _dot_sel_r(x, sel):
    hi, mid, lo = _split3(x)
    return _dot(hi, sel) + _dot(mid, sel) + _dot(lo, sel)


def _dot_hp(a, b):
    ah, al = _split2(a)
    bh, bl = _split2(b)
    return _dot(ah, bh) + _dot(ah, bl) + _dot(al, bh)


def _dot1(a, b):
    return _dot(a.astype(BF16), b.astype(BF16))


def _dot1_nt(a, b):
    return _dot_nt(a.astype(BF16), b.astype(BF16))


def _softplus(x):
    return jnp.maximum(x, 0.0) + jnp.log(1.0 + jnp.exp(-jnp.abs(x)))


def _silu(x):
    return x * jax.nn.sigmoid(x)


def _rms(x, w):
    return x * lax.rsqrt(jnp.mean(x * x, axis=-1, keepdims=True) + NORM_EPS) * w


def _inproj_body(x_ref, nw_ref, w_ref, o_ref):
    h = _rms(x_ref[...], nw_ref[...]).astype(BF16)
    o_ref[...] = _dot(h, w_ref[...])


def _in_proj(x2, nw, w):
    m = x2.shape[0]
    tm, tn = 256, N_PROJ // 2
    return pl.pallas_call(
        _inproj_body,
        grid=(N_PROJ // tn, m // tm),
        in_specs=[pl.BlockSpec((tm, D_MODEL), lambda j, i: (i, 0)),
                  pl.BlockSpec((1, D_MODEL), lambda j, i: (0, 0)),
                  pl.BlockSpec((D_MODEL, tn), lambda j, i: (0, j))],
        out_specs=pl.BlockSpec((tm, tn), lambda j, i: (i, j)),
        out_shape=jax.ShapeDtypeStruct((m, N_PROJ), F32),
        compiler_params=_cparams(("parallel", "parallel")),
        name="in_proj",
    )(x2, nw, w)


def _halo_specs(tl, width, col_block, nt, nrows):
    per = tl // SUBLANES
    last = nrows // SUBLANES - 1
    cur = pl.BlockSpec((tl, width), lambda b, i: (b * nt + i, col_block))
    prev = pl.BlockSpec((SUBLANES, width), lambda b, i: (jnp.maximum((b * nt + i) * per - 1, 0), col_block))
    nxt = pl.BlockSpec((SUBLANES, width), lambda b, i: (jnp.minimum((b * nt + i + 1) * per, last), col_block))
    return cur, prev, nxt


def _fill_halo(scr, cur_ref, prev_ref, next_ref):
    i = pl.program_id(1)
    tl = cur_ref.shape[0]
    scr[0:SUBLANES, :] = jnp.where(i > 0, prev_ref[...], 0.0)
    scr[SUBLANES:SUBLANES + tl, :] = cur_ref[...]
    scr[SUBLANES + tl:2 * SUBLANES + tl, :] = jnp.where(i < pl.num_programs(1) - 1, next_ref[...], 0.0)


def _conv_body(cur_ref, prev_ref, next_ref, w_ref, b_ref, o_ref, scr):
    _fill_halo(scr, cur_ref, prev_ref, next_ref)
    tl = cur_ref.shape[0]
    acc = b_ref[...] + w_ref[0:1, :] * scr[pl.ds(SUBLANES - 2, tl), :]
    for j in range(1, D_CONV):
        acc = acc + w_ref[j:j + 1, :] * scr[pl.ds(SUBLANES - 2 + j, tl), :]
    o_ref[...] = _silu(acc)


def _conv(proj, conv_w8, conv_b, bsz, seq):
    tl = min(512, seq)
    nt = seq // tl
    width = 1024
    cur, prev, nxt = _halo_specs(tl, width, C_XBC // width, nt, bsz * seq)
    return pl.pallas_call(
        _conv_body,
        grid=(bsz, nt),
        in_specs=[cur, prev, nxt,
                  pl.BlockSpec((SUBLANES, width), lambda b, i: (0, 0)),
                  pl.BlockSpec((1, width), lambda b, i: (0, 0))],
        out_specs=pl.BlockSpec((tl, width), lambda b, i: (b * nt + i, 0)),
        out_shape=jax.ShapeDtypeStruct((bsz * seq, width), F32),
        scratch_shapes=[pltpu.VMEM((tl + 2 * SUBLANES, width), F32)],
        compiler_params=_cparams(("parallel", "parallel")),
        name="ssm_conv",
    )(proj, proj, proj, conv_w8, conv_b)


def _ssd_body(reverse, *refs):
    if reverse:
        (xs_ref, bc_ref, dt_ref, dtb_ref, alog_ref, e_ref, yf_ref, z_ref, dsk_ref, nw_ref, o_ref, st_ref) = refs
    else:
        (xs_ref, bc_ref, dt_ref, dtb_ref, alog_ref, e_ref, o_ref, st_ref) = refs
    c = pl.program_id(1)

    @pl.when(c == 0)
    def _():
        st_ref[...] = jnp.zeros_like(st_ref)

    t = xs_ref.shape[0]
    row = lax.broadcasted_iota(jnp.int32, (t, t), 0)
    col = lax.broadcasted_iota(jnp.int32, (t, t), 1)
    tri = (col >= row) if reverse else (col <= row)
    tri_bf = jnp.where(tri, 1.0, 0.0).astype(BF16)
    dt_all = _softplus(dt_ref[...] + dtb_ref[...])
    a_all = dt_all * (-jnp.exp(alog_ref[...]))
    cum = _dot_sel_l(tri_bf, a_all)
    sel = e_ref[...]
    dt_e = _dot_sel_r(dt_all, sel)
    cum_e = _dot_sel_r(cum, sel)
    tot_e = cum_e[0:1, :] if reverse else cum_e[t - 1:t, :]
    cum_t = cum.T
    xs = xs_ref[...]
    xdt = xs * dt_e
    bc = bc_ref[...]
    lane = lax.broadcasted_iota(jnp.int32, (t, LANES), 1)
    first = lane < HEAD64
    ofs = SSM_HEADS if reverse else 0
    ys = []
    for g in range(2):
        bm = bc[:, g * SSM_STATE:(g + 1) * SSM_STATE]
        cm = bc[:, 256 + g * SSM_STATE:256 + (g + 1) * SSM_STATE].astype(BF16)
        cb = _dot_nt(cm, bm.astype(BF16))
        bm_t = bm.T.astype(BF16)
        for pr in range(2):
            p = g * 2 + pr
            sl = slice(p * LANES, (p + 1) * LANES)
            xdt_p = xdt[:, sl]
            xdt_bf = xdt_p.astype(BF16)
            halves = []
            for hh in range(2):
                ci = ofs + 2 * p + hh
                dec = jnp.exp(jnp.where(tri, cum[:, ci:ci + 1] - cum_t[ci:ci + 1, :], -jnp.inf))
                halves.append(_dot((cb * dec).astype(BF16), xdt_bf))
            y_diag = jnp.where(first, halves[0], halves[1])
            state = st_ref[p]
            cum_p = cum_e[:, sl]
            y_off = _dot(cm, state.astype(BF16)) * jnp.exp(cum_p)
            ys.append(y_diag + y_off)
            tot_p = tot_e[:, sl]
            loc = _dot(bm_t, (xdt_p * jnp.exp(tot_p - cum_p)).astype(BF16))
            st_ref[p] = state * jnp.exp(tot_p) + loc
    y = jnp.concatenate(ys, axis=1)
    if not reverse:
        o_ref[...] = y
        return
    y = y + yf_ref[...] + xs * dsk_ref[...]
    yg = y * _silu(z_ref[...])
    nw = nw_ref[...]
    half = BRANCH // 2
    o_ref[...] = jnp.concatenate(
        [_rms(yg[:, g * half:(g + 1) * half], nw[:, g * half:(g + 1) * half]) for g in range(2)], axis=1)


def _ssd(reverse, xbc, proj, dt_bias, a_log, sel, extra, bsz, seq):
    t = min(SSD_CHUNK, seq)
    nc = seq // t
    if reverse:
        rmap = lambda b, c: (b * nc + nc - 1 - c)
    else:
        rmap = lambda b, c: (b * nc + c)
    in_specs = [pl.BlockSpec((t, BRANCH), lambda b, c: (rmap(b, c), 0)),
                pl.BlockSpec((t, BRANCH), lambda b, c: (rmap(b, c), 1)),
                pl.BlockSpec((t, LANES), lambda b, c: (rmap(b, c), C_DT // LANES)),
                pl.BlockSpec((1, LANES), lambda b, c: (0, 0)),
                pl.BlockSpec((1, LANES), lambda b, c: (0, 0)),
                pl.BlockSpec((LANES, BRANCH), lambda b, c: (0, 0))]
    args = [xbc, xbc, proj, dt_bias, a_log, sel]
    if reverse:
        y_f, d_skip, norm_w = extra
        in_specs += [pl.BlockSpec((t, BRANCH), lambda b, c: (rmap(b, c), 0)),
                     pl.BlockSpec((t, BRANCH), lambda b, c: (rmap(b, c), C_Z // BRANCH)),
                     pl.BlockSpec((1, BRANCH), lambda b, c: (0, 0)),
                     pl.BlockSpec((1, BRANCH), lambda b, c: (0, 0))]
        args += [y_f, proj, d_skip, norm_w]
    return pl.pallas_call(
        functools.partial(_ssd_body, reverse),
        grid=(bsz, nc),
        in_specs=in_specs,
        out_specs=pl.BlockSpec((t, BRANCH), lambda b, c: (rmap(b, c), 0)),
        out_shape=jax.ShapeDtypeStruct((bsz * seq, BRANCH), F32),
        scratch_shapes=[pltpu.VMEM((4, SSM_STATE, LANES), F32)],
        compiler_params=_cparams(("parallel", "arbitrary")),
        name="ssd_bwd" if reverse else "ssd_fwd",
    )(*args)


def _rwkv_prep_body(cur_ref, prev_ref, next_ref, wcur_ref, wprev_ref, wnext_ref, mu_ref, muw_ref, w0_ref, w2_ref,
                    a0_ref, a2_ref, kk_ref, ka_ref, rk_ref, blk_ref, *outs):
    scr, wscr = outs[-2:]
    bonus_ref = outs[-3]
    o_refs = outs[:-3]
    _fill_halo(scr, cur_ref, prev_ref, next_ref)
    _fill_halo(wscr, wcur_ref, wprev_ref, wnext_ref)
    tl = cur_ref.shape[0]
    u0 = cur_ref[...]
    wa0 = wcur_ref[...]
    blk = blk_ref[...]

    def head_sum(x):
        return jnp.concatenate(
            [_dot_sel_r(x[:, j * LANES:(j + 1) * LANES], blk) for j in range(BRANCH // LANES)], axis=1)

    bonus = None
    for d in range(2):
        off = SUBLANES - 1 if d == 0 else SUBLANES + 1
        u = u0 + (scr[pl.ds(off, tl), :] - u0) * mu_ref[d:d + 1, :]
        wa = wa0 + (wscr[pl.ds(off, tl), :] - wa0) * muw_ref[d:d + 1, :]
        r, k, v = u[:, 0:BRANCH], u[:, BRANCH:2 * BRANCH], u[:, 2 * BRANCH:3 * BRANCH]
        wlin = w0_ref[d:d + 1, :] + _dot_hp(jnp.tanh(wa), w2_ref[d])
        w = -_softplus(-wlin) - 0.5
        lw = -jnp.exp(w)
        eta = jax.nn.sigmoid(a0_ref[...] + _dot_hp(wa, a2_ref[...]))
        kk = k * kk_ref[...]
        kk = kk / jnp.maximum(jnp.sqrt(head_sum(kk * kk)), 1e-12)
        k2 = k * (1.0 + (eta - 1.0) * ka_ref[...])
        o_refs[6 * d + 0][...] = r
        o_refs[6 * d + 1][...] = k2
        o_refs[6 * d + 2][...] = v
        o_refs[6 * d + 3][...] = lw
        o_refs[6 * d + 4][...] = -kk
        o_refs[6 * d + 5][...] = kk * eta
        bd = head_sum(r * k2 * rk_ref[...]) * v
        bonus = bd if bonus is None else bonus + bd
    bonus_ref[...] = bonus


def _rwkv_prep(proj, prm, bsz, seq):
    tl = min(256, seq)
    nt = seq // tl
    m = bsz * seq
    wide = 3 * BRANCH
    cur, prev, nxt = _halo_specs(tl, wide, C_RKV // wide, nt, m)
    wcur, wprev, wnxt = _halo_specs(tl, LANES, C_WA // LANES, nt, m)
    full = lambda shape: pl.BlockSpec(shape, lambda b, i: (0,) * len(shape))
    out_spec = pl.BlockSpec((tl, BRANCH), lambda b, i: (b * nt + i, 0))
    outs = pl.pallas_call(
        _rwkv_prep_body,
        grid=(bsz, nt),
        in_specs=[cur, prev, nxt, wcur, wprev, wnxt,
                  full((2, wide)), full((2, LANES)), full((2, BRANCH)), full((2, LANES, BRANCH)),
                  full((1, BRANCH)), full((LANES, BRANCH)), full((1, BRANCH)), full((1, BRANCH)),
                  full((1, BRANCH)), full((LANES, LANES))],
        out_specs=[out_spec] * 13,
        out_shape=[jax.ShapeDtypeStruct((m, BRANCH), F32)] * 13,
        scratch_shapes=[pltpu.VMEM((tl + 2 * SUBLANES, wide), F32), pltpu.VMEM((tl + 2 * SUBLANES, LANES), F32)],
        compiler_params=_cparams(("parallel", "parallel")),
        name="rwkv_prep",
    )(proj, proj, proj, proj, proj, proj, prm["mu_rkv"], prm["mu_wa"], prm["w0"], prm["w2"], prm["a0"], prm["a2"],
      prm["k_k"], prm["k_a"], prm["r_k"], prm["blk"])
    return outs[0:6], outs[6:12], outs[12]


def _wkv_body(reverse, *refs):
    if reverse:
        (r_ref, k_ref, v_ref, lw_ref, a_ref, b_ref, yf_ref, bonus_ref, g_ref, lnw_ref, lnb_ref, blk_ref,
         o_ref, s_ref) = refs
    else:
        (r_ref, k_ref, v_ref, lw_ref, a_ref, b_ref, o_ref, s_ref) = refs
    ci = pl.program_id(0)

    @pl.when(ci == 0)
    def _():
        s_ref[...] = jnp.zeros_like(s_ref)

    nb, c = r_ref.shape[0], r_ref.shape[1]
    c2 = 2 * c
    npair = BRANCH // LANES
    row = lax.broadcasted_iota(jnp.int32, (c, c), 0)
    col = lax.broadcasted_iota(jnp.int32, (c, c), 1)
    tri_bf = jnp.where((col >= row) if reverse else (col <= row), 1.0, 0.0).astype(BF16)
    lane = lax.broadcasted_iota(jnp.int32, (c, LANES), 1)
    first = lane < HEAD64
    row2 = lax.broadcasted_iota(jnp.int32, (c2, c2), 0)
    col2 = lax.broadcasted_iota(jnp.int32, (c2, c2), 1)
    same = jnp.where(row2 >= c, 1, 0) == jnp.where(col2 >= c, 1, 0)
    if reverse:
        strict, incl = same & (col2 > row2), same & (col2 >= row2)
    else:
        strict, incl = same & (col2 < row2), same & (col2 <= row2)
    diag = row2 == col2
    eye2 = jnp.where(diag, 1.0, 0.0)
    zero_blk = jnp.zeros((c2, LANES), F32)
    squarings = max(1, int(math.ceil(math.log2(c))) - 1)

    def stack(x):
        return jnp.concatenate([jnp.where(first, x, 0.0), jnp.where(first, 0.0, x)], axis=0)

    chains = [(bi, p) for bi in range(nb) for p in range(npair)]
    pre = []
    for bi in range(nb):
        r, k, v, lw, a, b = r_ref[bi], k_ref[bi], v_ref[bi], lw_ref[bi], a_ref[bi], b_ref[bi]
        cum = _dot_sel_l(tri_bf, lw)
        cum_prev = cum - lw
        tot = cum[0:1, :] if reverse else cum[c - 1:c, :]
        mid = cum[c // 2:c // 2 + 1, :]
        e_in = jnp.exp(mid - cum)
        e_end = jnp.exp(tot - cum)
        pre.append(dict(am=a * jnp.exp(cum_prev - mid), bm=b * e_in, km=k * e_in, rm=r * jnp.exp(cum - mid),
                        a0=a * jnp.exp(cum_prev), r0=r * jnp.exp(cum), bend=b * e_end, kend=k * e_end,
                        gdiag=jnp.exp(tot), v=v))

    def part(name, ch):
        bi, p = ch
        return pre[bi][name][:, p * LANES:(p + 1) * LANES]

    v_s = [stack(part("v", ch)) for ch in chains]
    prod = [_dot_nt(jnp.concatenate([stack(part("am", ch)), stack(part("rm", ch))], axis=0).astype(BF16),
                    jnp.concatenate([stack(part("bm", ch)), stack(part("km", ch))], axis=0).astype(BF16))
            for ch in chains]
    l_ab = [jnp.where(strict, x[0:c2, 0:c2], 0.0) for x in prod]
    l_ak = [jnp.where(strict, x[0:c2, c2:2 * c2], 0.0) for x in prod]
    a_rb = [jnp.where(incl, x[c2:2 * c2, 0:c2], 0.0) for x in prod]
    a_rk = [jnp.where(incl, x[c2:2 * c2, c2:2 * c2], 0.0) for x in prod]
    tinv = [eye2 + x for x in l_ab]
    pw = [x.astype(BF16) for x in l_ab]
    pw = [_dot(x, x).astype(BF16) for x in pw]
    lakv = [_dot1(x, y) for x, y in zip(l_ak, v_s)]
    for i in range(squarings):
        if i == squarings - 1:
            tinv = [t + _dot(t.astype(BF16), x) for t, x in zip(tinv, pw)]
        else:
            xs = [_dot(jnp.concatenate([t.astype(BF16), x], axis=0), x) for t, x in zip(tinv, pw)]
            tinv = [t + x[0:c2, :] for t, x in zip(tinv, xs)]
            pw = [x[c2:2 * c2, :].astype(BF16) for x in xs]
    au = [_dot(t.astype(BF16), jnp.concatenate([stack(part("a0", ch)), x], axis=1).astype(BF16))
          for t, x, ch in zip(tinv, lakv, chains)]
    big = []
    for n, ch in enumerate(chains):
        ends_t = jnp.concatenate([stack(part("bend", ch)), stack(part("kend", ch))], axis=0).T
        lhs3 = jnp.concatenate([jnp.concatenate([a_rb[n], a_rk[n]], axis=1), ends_t], axis=0).astype(BF16)
        rhs3 = jnp.concatenate([au[n], jnp.concatenate([zero_blk, v_s[n]], axis=1)], axis=0).astype(BF16)
        big.append(_dot(lhs3, rhs3))
    res = []
    for n, ch in enumerate(chains):
        r_hat = stack(part("r0", ch)) + big[n][0:c2, 0:LANES]
        gmat = jnp.where(diag, part("gdiag", ch), 0.0) + big[n][c2:c2 + LANES, 0:LANES]
        res.append(_dot_hp(jnp.concatenate([r_hat, gmat], axis=0), s_ref[n]))
    ys = []
    for n, ch in enumerate(chains):
        s_ref[n] = res[n][c2:c2 + LANES, :] + big[n][c2:c2 + LANES, LANES:2 * LANES]
        y_s = res[n][0:c2, :] + big[n][0:c2, LANES:2 * LANES]
        ys.append(y_s[0:c, :] + y_s[c:c2, :])
    for bi in range(nb):
        y = jnp.concatenate(ys[bi * npair:(bi + 1) * npair], axis=1)
        if not reverse:
            o_ref[bi] = y
            continue
        wkv = y + yf_ref[bi]
        blk = blk_ref[...]
        inv_n = 1.0 / HEAD64

        def head_mean(x):
            return jnp.concatenate(
                [_dot_sel_r(x[:, j * LANES:(j + 1) * LANES], blk) for j in range(npair)], axis=1) * inv_n

        cen = wkv - head_mean(wkv)
        gn = cen * lax.rsqrt(head_mean(cen * cen) + RWKV_GN_EPS) * lnw_ref[...] + lnb_ref[...]
        o_ref[bi] = (gn + bonus_ref[bi]) * _silu(g_ref[bi])


def _wkv(reverse, ops, extra, bsz, seq):
    c = min(WKV_CHUNK, seq)
    nc = seq // c
    npair = BRANCH // LANES
    rmap = (lambda ci: nc - 1 - ci) if reverse else (lambda ci: ci)
    as3d = lambda t: t.reshape(bsz, seq, t.shape[-1])
    tile = pl.BlockSpec((bsz, c, BRANCH), lambda ci: (0, rmap(ci), 0))
    in_specs = [tile] * 6
    args = [as3d(t) for t in ops]
    if reverse:
        y_f, bonus, proj, ln_w, ln_b, blk = extra
        in_specs += [tile, tile,
                     pl.BlockSpec((bsz, c, BRANCH), lambda ci: (0, rmap(ci), C_RG // BRANCH)),
                     pl.BlockSpec((1, BRANCH), lambda ci: (0, 0)),
                     pl.BlockSpec((1, BRANCH), lambda ci: (0, 0)),
                     pl.BlockSpec((LANES, LANES), lambda ci: (0, 0))]
        args += [as3d(y_f), as3d(bonus), as3d(proj), ln_w, ln_b, blk]
    out = pl.pallas_call(
        functools.partial(_wkv_body, reverse),
        grid=(nc,),
        in_specs=in_specs,
        out_specs=tile,
        out_shape=jax.ShapeDtypeStruct((bsz, seq, BRANCH), F32),
        scratch_shapes=[pltpu.VMEM((bsz * npair, LANES, LANES), F32)],
        compiler_params=_cparams(("arbitrary",)),
        name="wkv_bwd" if reverse else "wkv_fwd",
    )(*args)
    return out.reshape(bsz * seq, BRANCH)


def _rope(x, cos, sin_signed):
    width = x.shape[1]
    reps = width // LANES
    lane = lax.broadcasted_iota(jnp.int32, x.shape, 1)
    first_half = (lane & (HEAD64 - 1)) < (HEAD64 // 2)
    partner = jnp.where(first_half, pltpu.roll(x, width - HEAD64 // 2, 1), pltpu.roll(x, HEAD64 // 2, 1))
    cos_w = jnp.concatenate([cos] * reps, axis=1) if reps > 1 else cos
    sin_w = jnp.concatenate([sin_signed] * reps, axis=1) if reps > 1 else sin_signed
    return x * cos_w + partner * sin_w


def _attn_prep_body(dq_ref, dk_ref, dv_ref, gq_ref, gk_ref, gv_ref, cd_ref, sd_ref, cg_ref, sg_ref, qw_ref, kw_ref,
                    q1_ref, q2_ref, dko_ref, dvo_ref, gqo_ref, gko_ref, gvo_ref):
    cd, sd, cg, sg = cd_ref[...], sd_ref[...], cg_ref[...], sg_ref[...]
    q = _rope(dq_ref[...], cd, sd) * (HEAD64 ** -0.5 * LOG2E)
    lane = lax.broadcasted_iota(jnp.int32, q.shape, 1)
    comp0 = (lane & (LANES - 1)) < HEAD64
    q1_ref[...] = jnp.where(comp0, q, 0.0).astype(BF16)
    q2_ref[...] = jnp.where(comp0, 0.0, q).astype(BF16)
    dko_ref[...] = _rope(dk_ref[...], cd, sd).astype(BF16)

    def store_vt(v_ref, o_ref):
        vt = v_ref[...].T
        tl = vt.shape[1]
        for h in range(vt.shape[0] // LANES):
            o_ref[0, h, 0:LANES, :] = vt[h * LANES:(h + 1) * LANES, :].astype(BF16)
            o_ref[0, h, LANES:VT_ROWS, :] = jnp.ones((VT_ROWS - LANES, tl), BF16)

    store_vt(dv_ref, dvo_ref)

    def norm_heads(x, w):
        return jnp.concatenate(
            [_rms(x[:, j * LANES:(j + 1) * LANES], w) for j in range(x.shape[1] // LANES)], axis=1)

    gq = _rope(norm_heads(gq_ref[...], qw_ref[...]), cg, sg) * (LANES ** -0.5 * LOG2E)
    gqo_ref[...] = gq.astype(BF16)
    gko_ref[...] = _rope(norm_heads(gk_ref[...], kw_ref[...]), cg, sg).astype(BF16)
    store_vt(gv_ref, gvo_ref)


def _attn_prep(proj, tabs, qw, kw, bsz, seq):
    tl = min(256, seq)
    nt = seq // tl
    m = bsz * seq
    colspec = lambda width, off: pl.BlockSpec((tl, width), lambda b, i: (b * nt + i, off // width))
    tab = pl.BlockSpec((tl, LANES), lambda b, i: (i, 0))
    wspec = pl.BlockSpec((1, LANES), lambda b, i: (0, 0))
    o512 = pl.BlockSpec((tl, BRANCH), lambda b, i: (b * nt + i, 0))
    o256 = pl.BlockSpec((tl, 256), lambda b, i: (b * nt + i, 0))
    s512 = jax.ShapeDtypeStruct((m, BRANCH), BF16)
    s256 = jax.ShapeDtypeStruct((m, 256), BF16)
    vt_spec = lambda heads: pl.BlockSpec((1, heads, VT_ROWS, tl), lambda b, i: (b, 0, 0, i))
    vt_shape = lambda heads: jax.ShapeDtypeStruct((bsz, heads, VT_ROWS, seq), BF16)
    return pl.pallas_call(
        _attn_prep_body,
        grid=(bsz, nt),
        in_specs=[colspec(BRANCH, C_DQ), colspec(BRANCH, C_DK), colspec(BRANCH, C_DV), colspec(BRANCH, C_GQ),
                  colspec(256, C_GK), colspec(256, C_GV), tab, tab, tab, tab, wspec, wspec],
        out_specs=[o512, o512, o512, vt_spec(4), o512, o256, vt_spec(2)],
        out_shape=[s512, s512, s512, vt_shape(4), s512, s256, vt_shape(2)],
        compiler_params=_cparams(("parallel", "parallel")),
        name="attn_prep",
    )(proj, proj, proj, proj, proj, proj, *tabs, qw, kw)


def _flash_body(diff, scale_out, ck, *refs):
    if diff:
        qa_ref, qb_ref, k_ref, vt_ref, lam_ref, nw_ref, o_ref, s_scr, p_scr, acc_scr = refs
    else:
        qa_ref, qb_ref, k_ref, vt_ref, o_ref, s_scr, p_scr, acc_scr = refs
    tq = qa_ref.shape[0]
    q = jnp.concatenate([qa_ref[...], qb_ref[...]], axis=0)
    nchunk = k_ref.shape[0] // ck

    def scores(j, slot):
        off = pl.multiple_of(j * ck, ck)
        s = _dot_nt(k_ref[pl.ds(off, ck), :], q)
        s_scr[slot] = s
        return jnp.max(s, axis=0, keepdims=True)

    def pv(j, slot, alpha):
        off = pl.multiple_of(j * ck, ck)
        acc_scr[...] = alpha * acc_scr[...] + _dot(vt_ref[0, 0, :, pl.ds(off, ck)], p_scr[slot])

    def step(j, slot, m_prev, cmax, alpha_prev):
        cmax_next = scores(jnp.minimum(j + 1, nchunk - 1), 1 - slot)
        pv(jnp.maximum(j - 1, 0), 1 - slot, alpha_prev)
        m_new = jnp.maximum(m_prev, cmax)
        p_scr[slot] = jnp.exp2(s_scr[slot] - m_new).astype(BF16)
        return m_new, cmax_next, jnp.exp2(m_prev - m_new)

    def body(i, carry):
        m, cmax, alpha = carry
        m, cmax, alpha = step(2 * i, 0, m, cmax, alpha)
        return step(2 * i + 1, 1, m, cmax, alpha)

    acc_scr[...] = jnp.zeros_like(acc_scr)
    p_scr[1] = jnp.zeros(p_scr.shape[1:], BF16)
    init = (jnp.full((1, 2 * tq), -jnp.inf, F32), scores(0, 0), jnp.ones((1, 2 * tq), F32))
    _, _, alpha_last = lax.fori_loop(0, nchunk // 2, body, init)
    pv(nchunk - 1, 1, alpha_last)
    acc = acc_scr[...]
    o = (acc[0:LANES, :] / acc[LANES:LANES + 1, :]).T
    oa, ob = o[0:tq, :], o[tq:2 * tq, :]
    if diff:
        lp = lam_ref[...]
        lam = (jnp.exp(jnp.sum(lp[0:1, :] * lp[1:2, :], axis=1, keepdims=True))
               - jnp.exp(jnp.sum(lp[2:3, :] * lp[3:4, :], axis=1, keepdims=True)) + (1.0 - scale_out))
        o_ref[...] = _rms(oa - lam * ob, nw_ref[...]) * scale_out
    else:
        o_ref[...] = jnp.concatenate([oa, ob], axis=1)


def _flash(diff, qa, qb, k, vt, extra, lambda_init, bsz, seq):
    tq = min(FLASH_TQ, seq)
    ck = min(FLASH_CK, seq // 2)
    assert seq % (2 * ck) == 0 and seq % tq == 0
    nq = seq // tq
    groups = k.shape[1] // LANES
    qa_spec = pl.BlockSpec((tq, LANES), lambda b, g, i: (b * nq + i, g if diff else 2 * g))
    qb_spec = pl.BlockSpec((tq, LANES), lambda b, g, i: (b * nq + i, g if diff else 2 * g + 1))
    k_spec = pl.BlockSpec((seq, LANES), lambda b, g, i: (b, g))
    vt_spec = pl.BlockSpec((1, 1, VT_ROWS, seq), lambda b, g, i: (b, g, 0, 0))
    in_specs = [qa_spec, qb_spec, k_spec, vt_spec]
    args = [qa, qb, k, vt]
    if diff:
        in_specs += [pl.BlockSpec((4, HEAD64), lambda b, g, i: (0, 0)),
                     pl.BlockSpec((1, LANES), lambda b, g, i: (0, 0))]
        args += list(extra)
        out_spec = pl.BlockSpec((tq, LANES), lambda b, g, i: (b * nq + i, g))
    else:
        out_spec = pl.BlockSpec((tq, 2 * LANES), lambda b, g, i: (b * nq + i, g))
    return pl.pallas_call(
        functools.partial(_flash_body, diff, 1.0 - lambda_init, ck),
        grid=(bsz, groups, nq),
        in_specs=in_specs,
        out_specs=out_spec,
        out_shape=jax.ShapeDtypeStruct((bsz * seq, BRANCH), F32),
        scratch_shapes=[pltpu.VMEM((2, ck, 2 * tq), F32), pltpu.VMEM((2, ck, 2 * tq), BF16),
                        pltpu.VMEM((VT_ROWS, 2 * tq), F32)],
        compiler_params=_cparams(("parallel", "parallel", "parallel")),
        name="diff_attn" if diff else "gqa_attn",
    )(*args)


def _outproj_body(x_ref, ya_ref, yb_ref, oc_ref, dg_ref, od_ref, gg_ref, w_ref, pw_ref, o_ref):
    yc = oc_ref[...] * _silu(dg_ref[...])
    yd = od_ref[...] * _silu(gg_ref[...])
    mix = (_dot(ya_ref[...].astype(BF16), w_ref[0]) + _dot(yb_ref[...].astype(BF16), w_ref[1])
           + _dot(yc.astype(BF16), w_ref[2]) + _dot(yd.astype(BF16), w_ref[3]))
    o_ref[...] = x_ref[...] + _rms(mix, pw_ref[...])


def _out_proj(x2, y_a, y_b, o_c, o_d, proj, w4, pw):
    m = x2.shape[0]
    tm = 256
    rows = lambda width, cb: pl.BlockSpec((tm, width), lambda i: (i, cb))
    return pl.pallas_call(
        _outproj_body,
        grid=(m // tm,),
        in_specs=[rows(D_MODEL, 0), rows(BRANCH, 0), rows(BRANCH, 0), rows(BRANCH, 0), rows(BRANCH, C_DG // BRANCH),
                  rows(BRANCH, 0), rows(BRANCH, C_GG // BRANCH),
                  pl.BlockSpec((4, BRANCH, D_MODEL), lambda i: (0, 0, 0)),
                  pl.BlockSpec((1, D_MODEL), lambda i: (0, 0))],
        out_specs=rows(D_MODEL, 0),
        out_shape=jax.ShapeDtypeStruct((m, D_MODEL), F32),
        compiler_params=_cparams(("parallel",)),
        name="out_proj",
    )(x2, y_a, y_b, o_c, proj, o_d, proj, w4, pw)


def _reorder_w_in(w):
    parts = [w[:, 512:1536], w[:, 0:512], w[:, 1552:3088], w[:, 3216:3728], w[:, 3728:5776], w[:, 5776:7312],
             w[:, 3088:3216], w[:, 1536:1552], jnp.zeros((w.shape[0], N_PROJ - 7312), w.dtype)]
    return jnp.concatenate(parts, axis=1).astype(BF16)


def _rope_tables(seq):
    half = HEAD64 // 2
    inv = ROPE_THETA ** (-jnp.arange(half, dtype=F32) / half)
    t = jnp.arange(seq, dtype=jnp.int32)

    def tab(pos):
        ang = pos.astype(F32)[:, None] * inv[None, :]
        cos, sin = jnp.cos(ang), jnp.sin(ang)
        return jnp.concatenate([cos, cos], axis=1), jnp.concatenate([-sin, sin], axis=1)

    cp, sp = tab(t)
    cr, sr = tab(t // GRID_W)
    cc, sc = tab(t % GRID_W)
    return (jnp.concatenate([cp, cp], axis=1), jnp.concatenate([sp, sp], axis=1),
            jnp.concatenate([cr, cc], axis=1), jnp.concatenate([sr, sc], axis=1))


def _pad_lanes(v, width=LANES):
    v = v.reshape(1, -1)
    return jnp.pad(v, ((0, 0), (0, width - v.shape[1])))


def kernel(x, pre_norm_w, post_norm_w, w_in, w_out, conv_w, conv_b, ssm_a_log, ssm_dt_bias, ssm_d, ssm_norm_w,
           rwkv_mu, rwkv_w0, rwkv_w2, rwkv_a0, rwkv_a2, rwkv_k_k, rwkv_k_a, rwkv_r_k, rwkv_ln_w, rwkv_ln_b,
           diff_lambda, diff_norm_w, gqa_q_norm_w, gqa_k_norm_w):
    bsz, seq, _ = x.shape
    depth = w_in.shape[0]
    m = bsz * seq
    x2 = x.reshape(m, D_MODEL)
    tabs = _rope_tables(seq)
    lane_head = jnp.arange(LANES) // HEAD64
    blk = (lane_head[:, None] == lane_head[None, :]).astype(BF16)
    head_of_lane = jnp.arange(BRANCH) // HEAD64
    sels = [(jnp.arange(LANES)[:, None] == (d * SSM_HEADS + head_of_lane)[None, :]).astype(BF16) for d in range(2)]
    zpad = jnp.zeros((RWKV_RANK, BRANCH), F32)
    for i in range(depth):
        lambda_init = 0.8 - 0.6 * math.exp(-0.3 * i)
        proj = _in_proj(x2, pre_norm_w[i].reshape(1, -1), _reorder_w_in(w_in[i]))
        xbc = _conv(proj, jnp.pad(conv_w[i], ((0, SUBLANES - D_CONV), (0, 0))), conv_b[i].reshape(1, -1), bsz, seq)
        dt_bias = _pad_lanes(ssm_dt_bias[i])
        a_log = _pad_lanes(ssm_a_log[i])
        y_f = _ssd(False, xbc, proj, dt_bias, a_log, sels[0], None, bsz, seq)
        d_skip = jnp.repeat(ssm_d[i], HEAD64).reshape(1, -1)
        y_a = _ssd(True, xbc, proj, dt_bias, a_log, sels[1], (y_f, d_skip, ssm_norm_w[i].reshape(1, -1)), bsz, seq)
        prm = {
            "mu_rkv": rwkv_mu[i][:, :3 * BRANCH], "mu_wa": rwkv_mu[i][:, 3 * BRANCH:], "w0": rwkv_w0[i],
            "w2": jnp.concatenate([rwkv_w2[i], jnp.zeros((2, RWKV_RANK, BRANCH), F32)], axis=1),
            "a0": rwkv_a0[i].reshape(1, -1), "a2": jnp.concatenate([zpad, rwkv_a2[i]], axis=0),
            "k_k": rwkv_k_k[i].reshape(1, -1), "k_a": rwkv_k_a[i].reshape(1, -1),
            "r_k": rwkv_r_k[i].reshape(1, -1), "blk": blk,
        }
        ops_f, ops_b, bonus = _rwkv_prep(proj, prm, bsz, seq)
        wkv_f = _wkv(False, ops_f, None, bsz, seq)
        y_b = _wkv(True, ops_b, (wkv_f, bonus, proj, rwkv_ln_w[i].reshape(1, -1), rwkv_ln_b[i].reshape(1, -1), blk),
                   bsz, seq)
        q1, q2, dk, dv, gq, gk, gv = _attn_prep(proj, tabs, gqa_q_norm_w[i].reshape(1, -1),
                                                gqa_k_norm_w[i].reshape(1, -1), bsz, seq)
        o_c = _flash(True, q1, q2, dk, dv, (diff_lambda[i], diff_norm_w[i].reshape(1, -1)), lambda_init, bsz, seq)
        o_d = _flash(False, gq, gq, gk, gv, None, lambda_init, bsz, seq)
        x2 = _out_proj(x2, y_a, y_b, o_c, o_d, proj, w_out[i].reshape(4, BRANCH, D_MODEL).astype(BF16),
                       post_norm_w[i].reshape(1, -1))
    return x2.reshape(bsz, seq, D_MODEL)
```

```python
import functools
import math

import jax
import jax.numpy as jnp
from jax import lax
from jax.experimental import pallas as pl
from jax.experimental.pallas import tpu as pltpu

F32 = jnp.float32
BF16 = jnp.bfloat16

D_MODEL = 1024
GRID_W = 64
ROPE_THETA = 10000.0
NORM_EPS = 1e-6
BRANCH = 512
SSM_HEADS = 8
SSM_STATE = 128
D_CONV = 5
RWKV_RANK = 64
RWKV_GN_EPS = 64e-5
HEAD64 = 64
LANES = 128
SUBLANES = 8

C_XBC, C_Z = 0, 1024
C_RKV, C_RG = 1536, 3072
C_DQ, C_DK, C_DV, C_DG = 3584, 4096, 4608, 5120
C_GQ, C_GK, C_GV, C_GG = 5632, 6144, 6400, 6656
C_WA, C_DT = 7168, 7296
N_PROJ = 7424

SSD_CHUNK = 256
WKV_CHUNK = 64
WKV_SUB = 2
FLASH_TQ = 2048
FLASH_CK = 512
FLASH_UNROLL = 1
VT_ROWS = LANES + 16
LOG2E = 1.4426950408889634
VMEM_LIMIT = 48 * 1024 * 1024


def _cparams(sem):
    return pltpu.CompilerParams(dimension_semantics=sem, vmem_limit_bytes=VMEM_LIMIT)


def _dot(a, b):
    return jnp.dot(a, b, preferred_element_type=F32)


def _dot_nt(a, b):
    return lax.dot_general(a, b, (((1,), (1,)), ((), ())), preferred_element_type=F32)


def _split2(x):
    hi = x.astype(BF16)
    lo = (x - hi.astype(F32)).astype(BF16)
    return hi, lo


def _split3(x):
    hi = x.astype(BF16)
    r1 = x - hi.astype(F32)
    mid = r1.astype(BF16)
    lo = (r1 - mid.astype(F32)).astype(BF16)
    return hi, mid, lo


def _dot_sel_l(sel, x):
    hi, mid, lo = _split3(x)
    return _dot(sel, hi) + _dot(sel, mid) + _dot(sel, lo)


def _dot_sel_r(x, sel):
    hi, mid, lo = _split3(x)
    return _dot(hi, sel) + _dot(mid, sel) + _dot(lo, sel)


def _dot_sel_r2(x, sel):
    hi, lo = _split2(x)
    return _dot(hi, sel) + _dot(lo, sel)


def _dot_hp(a, b):
    ah, al = _split2(a)
    bh, bl = _split2(b)
    return _dot(ah, bh) + _dot(ah, bl) + _dot(al, bh)


def _dot1(a, b):
    return _dot(a.astype(BF16), b.astype(BF16))


def _dot1_nt(a, b):
    return _dot_nt(a.astype(BF16), b.astype(BF16))


def _softplus(x):
    return jnp.maximum(x, 0.0) + jnp.log(1.0 + jnp.exp(-jnp.abs(x)))


def _silu(x):
    return x * jax.nn.sigmoid(x)


def _rms(x, w):
    return x * lax.rsqrt(jnp.mean(x * x, axis=-1, keepdims=True) + NORM_EPS) * w


def _inproj_body(x_ref, nw_ref, w_ref, o_ref):
    h = _rms(x_ref[...], nw_ref[...]).astype(BF16)
    o_ref[...] = _dot(h, w_ref[...])


def _in_proj(x2, nw, w):
    m = x2.shape[0]
    tm, tn = 256, N_PROJ // 2
    return pl.pallas_call(
        _inproj_body,
        grid=(N_PROJ // tn, m // tm),
        in_specs=[pl.BlockSpec((tm, D_MODEL), lambda j, i: (i, 0)),
                  pl.BlockSpec((1, D_MODEL), lambda j, i: (0, 0)),
                  pl.BlockSpec((D_MODEL, tn), lambda j, i: (0, j))],
        out_specs=pl.BlockSpec((tm, tn), lambda j, i: (i, j)),
        out_shape=jax.ShapeDtypeStruct((m, N_PROJ), F32),
        compiler_params=_cparams(("parallel", "parallel")),
        name="in_proj",
    )(x2, nw, w)


def _halo_specs(tl, width, col_block, nt, nrows):
    per = tl // SUBLANES
    last = nrows // SUBLANES - 1
    cur = pl.BlockSpec((tl, width), lambda b, i: (b * nt + i, col_block))
    prev = pl.BlockSpec((SUBLANES, width), lambda b, i: (jnp.maximum((b * nt + i) * per - 1, 0), col_block))
    nxt = pl.BlockSpec((SUBLANES, width), lambda b, i: (jnp.minimum((b * nt + i + 1) * per, last), col_block))
    return cur, prev, nxt


def _fill_halo(scr, cur_ref, prev_ref, next_ref):
    i = pl.program_id(1)
    tl = cur_ref.shape[0]
    scr[0:SUBLANES, :] = jnp.where(i > 0, prev_ref[...], 0.0)
    scr[SUBLANES:SUBLANES + tl, :] = cur_ref[...]
    scr[SUBLANES + tl:2 * SUBLANES + tl, :] = jnp.where(i < pl.num_programs(1) - 1, next_ref[...], 0.0)


def _conv_body(cur_ref, prev_ref, next_ref, w_ref, b_ref, o_ref, scr):
    _fill_halo(scr, cur_ref, prev_ref, next_ref)
    tl = cur_ref.shape[0]
    acc = b_ref[...] + w_ref[0:1, :] * scr[pl.ds(SUBLANES - 2, tl), :]
    for j in range(1, D_CONV):
        acc = acc + w_ref[j:j + 1, :] * scr[pl.ds(SUBLANES - 2 + j, tl), :]
    o_ref[...] = _silu(acc)


def _conv(proj, conv_w8, conv_b, bsz, seq):
    tl = min(512, seq)
    nt = seq // tl
    width = 1024
    cur, prev, nxt = _halo_specs(tl, width, C_XBC // width, nt, bsz * seq)
    return pl.pallas_call(
        _conv_body,
        grid=(bsz, nt),
        in_specs=[cur, prev, nxt,
                  pl.BlockSpec((SUBLANES, width), lambda b, i: (0, 0)),
                  pl.BlockSpec((1, width), lambda b, i: (0, 0))],
        out_specs=pl.BlockSpec((tl, width), lambda b, i: (b * nt + i, 0)),
        out_shape=jax.ShapeDtypeStruct((bsz * seq, width), F32),
        scratch_shapes=[pltpu.VMEM((tl + 2 * SUBLANES, width), F32)],
        compiler_params=_cparams(("parallel", "parallel")),
        name="ssm_conv",
    )(proj, proj, proj, conv_w8, conv_b)


def _ssd_body(reverse, *refs):
    if reverse:
        (xs_ref, bc_ref, dt_ref, dtb_ref, alog_ref, e_ref, yf_ref, z_ref, dsk_ref, nw_ref, o_ref, st_ref) = refs
    else:
        (xs_ref, bc_ref, dt_ref, dtb_ref, alog_ref, e_ref, o_ref, st_ref) = refs
    c = pl.program_id(1)

    @pl.when(c == 0)
    def _():
        st_ref[...] = jnp.zeros_like(st_ref)

    t = xs_ref.shape[0]
    row = lax.broadcasted_iota(jnp.int32, (t, t), 0)
    col = lax.broadcasted_iota(jnp.int32, (t, t), 1)
    tri = (col >= row) if reverse else (col <= row)
    tri_bf = jnp.where(tri, 1.0, 0.0).astype(BF16)
    dt_all = _softplus(dt_ref[...] + dtb_ref[...])
    a_all = dt_all * (-jnp.exp(alog_ref[...]))
    cum = _dot_sel_l(tri_bf, a_all)
    sel = e_ref[...]
    dt_e = _dot_sel_r(dt_all, sel)
    cum_e = _dot_sel_r(cum, sel)
    tot_e = cum_e[0:1, :] if reverse else cum_e[t - 1:t, :]
    cum_t = cum.T
    xs = xs_ref[...]
    xdt = xs * dt_e
    bc = bc_ref[...]
    lane = lax.broadcasted_iota(jnp.int32, (t, LANES), 1)
    first = lane < HEAD64
    ofs = SSM_HEADS if reverse else 0
    ys = []
    for g in range(2):
        bm = bc[:, g * SSM_STATE:(g + 1) * SSM_STATE]
        cm = bc[:, 256 + g * SSM_STATE:256 + (g + 1) * SSM_STATE].astype(BF16)
        cb = _dot_nt(cm, bm.astype(BF16))
        bm_t = bm.T.astype(BF16)
        for pr in range(2):
            p = g * 2 + pr
            sl = slice(p * LANES, (p + 1) * LANES)
            xdt_p = xdt[:, sl]
            xdt_bf = xdt_p.astype(BF16)
            halves = []
            for hh in range(2):
                ci = ofs + 2 * p + hh
                dec = jnp.exp(jnp.where(tri, cum[:, ci:ci + 1] - cum_t[ci:ci + 1, :], -jnp.inf))
                halves.append(_dot((cb * dec).astype(BF16), xdt_bf))
            y_diag = jnp.where(first, halves[0], halves[1])
            state = st_ref[p]
            cum_p = cum_e[:, sl]
            y_off = _dot(cm, state.astype(BF16)) * jnp.exp(cum_p)
            ys.append(y_diag + y_off)
            tot_p = tot_e[:, sl]
            loc = _dot(bm_t, (xdt_p * jnp.exp(tot_p - cum_p)).astype(BF16))
            st_ref[p] = state * jnp.exp(tot_p) + loc
    y = jnp.concatenate(ys, axis=1)
    if not reverse:
        o_ref[...] = y
        return
    y = y + yf_ref[...] + xs * dsk_ref[...]
    yg = y * _silu(z_ref[...])
    nw = nw_ref[...]
    half = BRANCH // 2
    o_ref[...] = jnp.concatenate(
        [_rms(yg[:, g * half:(g + 1) * half], nw[:, g * half:(g + 1) * half]) for g in range(2)], axis=1)


def _ssd(reverse, xbc, proj, dt_bias, a_log, sel, extra, bsz, seq):
    t = min(SSD_CHUNK, seq)
    nc = seq // t
    if reverse:
        rmap = lambda b, c: (b * nc + nc - 1 - c)
    else:
        rmap = lambda b, c: (b * nc + c)
    in_specs = [pl.BlockSpec((t, BRANCH), lambda b, c: (rmap(b, c), 0)),
                pl.BlockSpec((t, BRANCH), lambda b, c: (rmap(b, c), 1)),
                pl.BlockSpec((t, LANES), lambda b, c: (rmap(b, c), C_DT // LANES)),
                pl.BlockSpec((1, LANES), lambda b, c: (0, 0)),
                pl.BlockSpec((1, LANES), lambda b, c: (0, 0)),
                pl.BlockSpec((LANES, BRANCH), lambda b, c: (0, 0))]
    args = [xbc, xbc, proj, dt_bias, a_log, sel]
    if reverse:
        y_f, d_skip, norm_w = extra
        in_specs += [pl.BlockSpec((t, BRANCH), lambda b, c: (rmap(b, c), 0)),
                     pl.BlockSpec((t, BRANCH), lambda b, c: (rmap(b, c), C_Z // BRANCH)),
                     pl.BlockSpec((1, BRANCH), lambda b, c: (0, 0)),
                     pl.BlockSpec((1, BRANCH), lambda b, c: (0, 0))]
        args += [y_f, proj, d_skip, norm_w]
    return pl.pallas_call(
        functools.partial(_ssd_body, reverse),
        grid=(bsz, nc),
        in_specs=in_specs,
        out_specs=pl.BlockSpec((t, BRANCH), lambda b, c: (rmap(b, c), 0)),
        out_shape=jax.ShapeDtypeStruct((bsz * seq, BRANCH), F32),
        scratch_shapes=[pltpu.VMEM((4, SSM_STATE, LANES), F32)],
        compiler_params=_cparams(("parallel", "arbitrary")),
        name="ssd_bwd" if reverse else "ssd_fwd",
    )(*args)


def _rwkv_prep_body(cur_ref, prev_ref, next_ref, wcur_ref, wprev_ref, wnext_ref, mu_ref, muw_ref, w0_ref, w2_ref,
                    a0_ref, a2_ref, kk_ref, ka_ref, rk_ref, blk_ref, *outs):
    scr, wscr = outs[-2:]
    bonus_ref = outs[-3]
    o_refs = outs[:-3]
    _fill_halo(scr, cur_ref, prev_ref, next_ref)
    _fill_halo(wscr, wcur_ref, wprev_ref, wnext_ref)
    tl = cur_ref.shape[0]
    u0 = cur_ref[...]
    wa0 = wcur_ref[...]
    blk = blk_ref[...]

    def head_sum(x):
        return jnp.concatenate(
            [_dot_sel_r2(x[:, j * LANES:(j + 1) * LANES], blk) for j in range(BRANCH // LANES)], axis=1)

    bonus = None
    for d in range(2):
        off = SUBLANES - 1 if d == 0 else SUBLANES + 1
        u = u0 + (scr[pl.ds(off, tl), :] - u0) * mu_ref[d:d + 1, :]
        wa = wa0 + (wscr[pl.ds(off, tl), :] - wa0) * muw_ref[d:d + 1, :]
        r, k, v = u[:, 0:BRANCH], u[:, BRANCH:2 * BRANCH], u[:, 2 * BRANCH:3 * BRANCH]
        wlin = w0_ref[d:d + 1, :] + _dot_hp(jnp.tanh(wa), w2_ref[d])
        w = -_softplus(-wlin) - 0.5
        lw = -jnp.exp(w)
        eta = jax.nn.sigmoid(a0_ref[...] + _dot_hp(wa, a2_ref[...]))
        kk = k * kk_ref[...]
        kk = kk / jnp.maximum(jnp.sqrt(head_sum(kk * kk)), 1e-12)
        k2 = k * (1.0 + (eta - 1.0) * ka_ref[...])
        o_refs[6 * d + 0][...] = r
        o_refs[6 * d + 1][...] = k2
        o_refs[6 * d + 2][...] = v
        o_refs[6 * d + 3][...] = lw
        o_refs[6 * d + 4][...] = -kk
        o_refs[6 * d + 5][...] = kk * eta
        bd = head_sum(r * k2 * rk_ref[...]) * v
        bonus = bd if bonus is None else bonus + bd
    bonus_ref[...] = bonus


def _rwkv_prep(proj, prm, bsz, seq):
    tl = min(256, seq)
    nt = seq // tl
    m = bsz * seq
    wide = 3 * BRANCH
    cur, prev, nxt = _halo_specs(tl, wide, C_RKV // wide, nt, m)
    wcur, wprev, wnxt = _halo_specs(tl, LANES, C_WA // LANES, nt, m)
    full = lambda shape: pl.BlockSpec(shape, lambda b, i: (0,) * len(shape))
    out_spec = pl.BlockSpec((tl, BRANCH), lambda b, i: (b * nt + i, 0))
    outs = pl.pallas_call(
        _rwkv_prep_body,
        grid=(bsz, nt),
        in_specs=[cur, prev, nxt, wcur, wprev, wnxt,
                  full((2, wide)), full((2, LANES)), full((2, BRANCH)), full((2, LANES, BRANCH)),
                  full((1, BRANCH)), full((LANES, BRANCH)), full((1, BRANCH)), full((1, BRANCH)),
                  full((1, BRANCH)), full((LANES, LANES))],
        out_specs=[out_spec] * 13,
        out_shape=[jax.ShapeDtypeStruct((m, BRANCH), F32)] * 13,
        scratch_shapes=[pltpu.VMEM((tl + 2 * SUBLANES, wide), F32), pltpu.VMEM((tl + 2 * SUBLANES, LANES), F32)],
        compiler_params=_cparams(("parallel", "parallel")),
        name="rwkv_prep",
    )(proj, proj, proj, proj, proj, proj, prm["mu_rkv"], prm["mu_wa"], prm["w0"], prm["w2"], prm["a0"], prm["a2"],
      prm["k_k"], prm["k_a"], prm["r_k"], prm["blk"])
    return outs[0:6], outs[6:12], outs[12]


def _wkv_body(reverse, *refs):
    if reverse:
        (r_ref, k_ref, v_ref, lw_ref, a_ref, b_ref, yf_ref, bonus_ref, g_ref, lnw_ref, lnb_ref, blk_ref,
         o_ref, s_ref) = refs
    else:
        (r_ref, k_ref, v_ref, lw_ref, a_ref, b_ref, o_ref, s_ref) = refs
    ci = pl.program_id(0)

    @pl.when(ci == 0)
    def _():
        s_ref[...] = jnp.zeros_like(s_ref)

    nb, rows = r_ref.shape[0], r_ref.shape[1]
    c = min(WKV_CHUNK, rows)
    nsub = rows // c
    c2 = 2 * c
    npair = BRANCH // LANES
    row = lax.broadcasted_iota(jnp.int32, (c, c), 0)
    col = lax.broadcasted_iota(jnp.int32, (c, c), 1)
    tri_bf = jnp.where((col >= row) if reverse else (col <= row), 1.0, 0.0).astype(BF16)
    lane = lax.broadcasted_iota(jnp.int32, (c, LANES), 1)
    first = lane < HEAD64
    row2 = lax.broadcasted_iota(jnp.int32, (c2, c2), 0)
    col2 = lax.broadcasted_iota(jnp.int32, (c2, c2), 1)
    same = jnp.where(row2 >= c, 1, 0) == jnp.where(col2 >= c, 1, 0)
    if reverse:
        strict, incl = same & (col2 > row2), same & (col2 >= row2)
    else:
        strict, incl = same & (col2 < row2), same & (col2 <= row2)
    diag = row2 == col2
    eye2 = jnp.where(diag, 1.0, 0.0)
    zero_blk = jnp.zeros((c2, LANES), F32)
    squarings = max(1, int(math.ceil(math.log2(c))) - 1)

    def stack(x):
        return jnp.concatenate([jnp.where(first, x, 0.0), jnp.where(first, 0.0, x)], axis=0)

    subs = list(range(nsub))[::-1] if reverse else list(range(nsub))
    chains = [(bi, sub, p) for sub in subs for bi in range(nb) for p in range(npair)]
    pre = {}
    for bi, sub in [(bi, sub) for sub in subs for bi in range(nb)]:
        rs = slice(sub * c, (sub + 1) * c)
        r, k, v, lw, a, b = (ref[bi, rs, :] for ref in (r_ref, k_ref, v_ref, lw_ref, a_ref, b_ref))
        cum = _dot_sel_l(tri_bf, lw)
        cum_prev = cum - lw
        tot = cum[0:1, :] if reverse else cum[c - 1:c, :]
        mid = cum[c // 2:c // 2 + 1, :]
        e_in = jnp.exp(mid - cum)
        e_end = jnp.exp(tot - cum)
        pre[bi, sub] = dict(am=a * jnp.exp(cum_prev - mid), bm=b * e_in, km=k * e_in, rm=r * jnp.exp(cum - mid),
                            a0=a * jnp.exp(cum_prev), r0=r * jnp.exp(cum), bend=b * e_end, kend=k * e_end,
                            gdiag=jnp.exp(tot), v=v)

    def part(name, ch):
        bi, sub, p = ch
        return pre[bi, sub][name][:, p * LANES:(p + 1) * LANES]

    v_s = [stack(part("v", ch)) for ch in chains]
    prod = [_dot_nt(jnp.concatenate([stack(part("am", ch)), stack(part("rm", ch))], axis=0).astype(BF16),
                    jnp.concatenate([stack(part("bm", ch)), stack(part("km", ch))], axis=0).astype(BF16))
            for ch in chains]
    l_ab = [jnp.where(strict, x[0:c2, 0:c2], 0.0) for x in prod]
    l_ak = [jnp.where(strict, x[0:c2, c2:2 * c2], 0.0) for x in prod]
    a_rb = [jnp.where(incl, x[c2:2 * c2, 0:c2], 0.0) for x in prod]
    a_rk = [jnp.where(incl, x[c2:2 * c2, c2:2 * c2], 0.0) for x in prod]
    tinv = [eye2 + x for x in l_ab]
    pw = [x.astype(BF16) for x in l_ab]
    pw = [_dot(x, x).astype(BF16) for x in pw]
    lakv = [_dot1(x, y) for x, y in zip(l_ak, v_s)]
    for i in range(squarings):
        if i == squarings - 1:
            tinv = [t + _dot(t.astype(BF16), x) for t, x in zip(tinv, pw)]
        else:
            xs = [_dot(jnp.concatenate([t.astype(BF16), x], axis=0), x) for t, x in zip(tinv, pw)]
            tinv = [t + x[0:c2, :] for t, x in zip(tinv, xs)]
            pw = [x[c2:2 * c2, :].astype(BF16) for x in xs]
    au = [_dot(t.astype(BF16), jnp.concatenate([stack(part("a0", ch)), x], axis=1).astype(BF16))
          for t, x, ch in zip(tinv, lakv, chains)]
    big = []
    for n, ch in enumerate(chains):
        ends_t = jnp.concatenate([stack(part("bend", ch)), stack(part("kend", ch))], axis=0).T
        lhs3 = jnp.concatenate([jnp.concatenate([a_rb[n], a_rk[n]], axis=1), ends_t], axis=0).astype(BF16)
        rhs3 = jnp.concatenate([au[n], jnp.concatenate([zero_blk, v_s[n]], axis=1)], axis=0).astype(BF16)
        big.append(_dot(lhs3, rhs3))
    lhs4 = []
    for n, ch in enumerate(chains):
        r_hat = stack(part("r0", ch)) + big[n][0:c2, 0:LANES]
        gmat = jnp.where(diag, part("gdiag", ch), 0.0) + big[n][c2:c2 + LANES, 0:LANES]
        g_hi, g_lo = _split2(gmat)
        lhs4.append((jnp.concatenate([r_hat.astype(BF16), g_hi], axis=0), g_lo))
    ys = {}
    per_sub = nb * npair
    for si, sub in enumerate(subs):
        idx = range(si * per_sub, (si + 1) * per_sub)
        st = [_split2(s_ref[n - si * per_sub]) for n in idx]
        res = [_dot(lhs4[n][0], s_hi) for n, (s_hi, s_lo) in zip(idx, st)]
        fine = [_dot(lhs4[n][0][c2:c2 + LANES, :], s_lo) + _dot(lhs4[n][1], s_hi) for n, (s_hi, s_lo) in zip(idx, st)]
        for n, x, f in zip(idx, res, fine):
            s_ref[n - si * per_sub] = x[c2:c2 + LANES, :] + f + big[n][c2:c2 + LANES, LANES:2 * LANES]
            y_s = x[0:c2, :] + big[n][0:c2, LANES:2 * LANES]
            ys[chains[n]] = y_s[0:c, :] + y_s[c:c2, :]
    for bi, sub in [(bi, sub) for sub in subs for bi in range(nb)]:
        rs = slice(sub * c, (sub + 1) * c)
        y = jnp.concatenate([ys[bi, sub, p] for p in range(npair)], axis=1)
        if not reverse:
            o_ref[bi, rs, :] = y
            continue
        wkv = y + yf_ref[bi, rs, :]
        blk = blk_ref[...]
        inv_n = 1.0 / HEAD64

        def head_mean(x):
            return jnp.concatenate(
                [_dot_sel_r2(x[:, j * LANES:(j + 1) * LANES], blk) for j in range(npair)], axis=1) * inv_n

        cen = wkv - head_mean(wkv)
        gn = cen * lax.rsqrt(head_mean(cen * cen) + RWKV_GN_EPS) * lnw_ref[...] + lnb_ref[...]
        o_ref[bi, rs, :] = (gn + bonus_ref[bi, rs, :]) * _silu(g_ref[bi, rs, :])


def _wkv(reverse, ops, extra, bsz, seq):
    c = min(WKV_CHUNK * WKV_SUB, seq)
    nc = seq // c
    npair = BRANCH // LANES
    rmap = (lambda ci: nc - 1 - ci) if reverse else (lambda ci: ci)
    as3d = lambda t: t.reshape(bsz, seq, t.shape[-1])
    tile = pl.BlockSpec((bsz, c, BRANCH), lambda ci: (0, rmap(ci), 0))
    in_specs = [tile] * 6
    args = [as3d(t) for t in ops]
    if reverse:
        y_f, bonus, proj, ln_w, ln_b, blk = extra
        in_specs += [tile, tile,
                     pl.BlockSpec((bsz, c, BRANCH), lambda ci: (0, rmap(ci), C_RG // BRANCH)),
                     pl.BlockSpec((1, BRANCH), lambda ci: (0, 0)),
                     pl.BlockSpec((1, BRANCH), lambda ci: (0, 0)),
                     pl.BlockSpec((LANES, LANES), lambda ci: (0, 0))]
        args += [as3d(y_f), as3d(bonus), as3d(proj), ln_w, ln_b, blk]
    out = pl.pallas_call(
        functools.partial(_wkv_body, reverse),
        grid=(nc,),
        in_specs=in_specs,
        out_specs=tile,
        out_shape=jax.ShapeDtypeStruct((bsz, seq, BRANCH), F32),
        scratch_shapes=[pltpu.VMEM((bsz * npair, LANES, LANES), F32)],
        compiler_params=_cparams(("arbitrary",)),
        name="wkv_bwd" if reverse else "wkv_fwd",
    )(*args)
    return out.reshape(bsz * seq, BRANCH)


def _rope(x, cos, sin_signed):
    width = x.shape[1]
    reps = width // LANES
    lane = lax.broadcasted_iota(jnp.int32, x.shape, 1)
    first_half = (lane & (HEAD64 - 1)) < (HEAD64 // 2)
    partner = jnp.where(first_half, pltpu.roll(x, width - HEAD64 // 2, 1), pltpu.roll(x, HEAD64 // 2, 1))
    cos_w = jnp.concatenate([cos] * reps, axis=1) if reps > 1 else cos
    sin_w = jnp.concatenate([sin_signed] * reps, axis=1) if reps > 1 else sin_signed
    return x * cos_w + partner * sin_w


def _attn_prep_body(dq_ref, dk_ref, dv_ref, gq_ref, gk_ref, gv_ref, cd_ref, sd_ref, cg_ref, sg_ref, qw_ref, kw_ref,
                    q1_ref, q2_ref, dko_ref, dvo_ref, gqo_ref, gko_ref, gvo_ref):
    cd, sd, cg, sg = cd_ref[...], sd_ref[...], cg_ref[...], sg_ref[...]
    q = _rope(dq_ref[...], cd, sd) * (HEAD64 ** -0.5 * LOG2E)
    lane = lax.broadcasted_iota(jnp.int32, q.shape, 1)
    comp0 = (lane & (LANES - 1)) < HEAD64
    q1_ref[...] = jnp.where(comp0, q, 0.0).astype(BF16)
    q2_ref[...] = jnp.where(comp0, 0.0, q).astype(BF16)
    dko_ref[...] = _rope(dk_ref[...], cd, sd).astype(BF16)

    def store_vt(v_ref, o_ref):
        vt = v_ref[...].T
        tl = vt.shape[1]
        for h in range(vt.shape[0] // LANES):
            o_ref[0, h, 0:LANES, :] = vt[h * LANES:(h + 1) * LANES, :].astype(BF16)
            o_ref[0, h, LANES:VT_ROWS, :] = jnp.ones((VT_ROWS - LANES, tl), BF16)

    store_vt(dv_ref, dvo_ref)

    def norm_heads(x, w):
        return jnp.concatenate(
            [_rms(x[:, j * LANES:(j + 1) * LANES], w) for j in range(x.shape[1] // LANES)], axis=1)

    gq = _rope(norm_heads(gq_ref[...], qw_ref[...]), cg, sg) * (LANES ** -0.5 * LOG2E)
    gqo_ref[...] = gq.astype(BF16)
    gko_ref[...] = _rope(norm_heads(gk_ref[...], kw_ref[...]), cg, sg).astype(BF16)
    store_vt(gv_ref, gvo_ref)


def _attn_prep(proj, tabs, qw, kw, bsz, seq):
    tl = min(256, seq)
    nt = seq // tl
    m = bsz * seq
    colspec = lambda width, off: pl.BlockSpec((tl, width), lambda b, i: (b * nt + i, off // width))
    tab = pl.BlockSpec((tl, LANES), lambda b, i: (i, 0))
    wspec = pl.BlockSpec((1, LANES), lambda b, i: (0, 0))
    o512 = pl.BlockSpec((tl, BRANCH), lambda b, i: (b * nt + i, 0))
    o256 = pl.BlockSpec((tl, 256), lambda b, i: (b * nt + i, 0))
    s512 = jax.ShapeDtypeStruct((m, BRANCH), BF16)
    s256 = jax.ShapeDtypeStruct((m, 256), BF16)
    vt_spec = lambda heads: pl.BlockSpec((1, heads, VT_ROWS, tl), lambda b, i: (b, 0, 0, i))
    vt_shape = lambda heads: jax.ShapeDtypeStruct((bsz, heads, VT_ROWS, seq), BF16)
    return pl.pallas_call(
        _attn_prep_body,
        grid=(bsz, nt),
        in_specs=[colspec(BRANCH, C_DQ), colspec(BRANCH, C_DK), colspec(BRANCH, C_DV), colspec(BRANCH, C_GQ),
                  colspec(256, C_GK), colspec(256, C_GV), tab, tab, tab, tab, wspec, wspec],
        out_specs=[o512, o512, o512, vt_spec(4), o512, o256, vt_spec(2)],
        out_shape=[s512, s512, s512, vt_shape(4), s512, s256, vt_shape(2)],
        compiler_params=_cparams(("parallel", "parallel")),
        name="attn_prep",
    )(proj, proj, proj, proj, proj, proj, *tabs, qw, kw)


def _flash_body(diff, scale_out, ck, *refs):
    if diff:
        qa_ref, qb_ref, k_ref, vt_ref, lam_ref, nw_ref, o_ref, s_scr, p_scr, acc_scr = refs
    else:
        qa_ref, qb_ref, k_ref, vt_ref, o_ref, s_scr, p_scr, acc_scr = refs
    tq = qa_ref.shape[0]
    q = jnp.concatenate([qa_ref[...], qb_ref[...]], axis=0)
    nchunk = k_ref.shape[0] // ck

    def scores(j, slot):
        off = pl.multiple_of(j * ck, ck)
        s = _dot_nt(k_ref[pl.ds(off, ck), :], q)
        s_scr[slot] = s
        return jnp.max(s, axis=0, keepdims=True)

    def pv(j, slot, alpha):
        off = pl.multiple_of(j * ck, ck)
        acc_scr[...] = alpha * acc_scr[...] + _dot(vt_ref[0, 0, :, pl.ds(off, ck)], p_scr[slot])

    def step(j, slot, m_prev, cmax, alpha_prev):
        cmax_next = scores(jnp.minimum(j + 1, nchunk - 1), 1 - slot)
        pv(jnp.maximum(j - 1, 0), 1 - slot, alpha_prev)
        m_new = jnp.maximum(m_prev, cmax)
        p_scr[slot] = jnp.exp2(s_scr[slot] - m_new).astype(BF16)
        return m_new, cmax_next, jnp.exp2(m_prev - m_new)

    def body(i, carry):
        m, cmax, alpha = carry
        m, cmax, alpha = step(2 * i, 0, m, cmax, alpha)
        return step(2 * i + 1, 1, m, cmax, alpha)

    acc_scr[...] = jnp.zeros_like(acc_scr)
    p_scr[1] = jnp.zeros(p_scr.shape[1:], BF16)
    init = (jnp.full((1, 2 * tq), -jnp.inf, F32), scores(0, 0), jnp.ones((1, 2 * tq), F32))
    _, _, alpha_last = lax.fori_loop(0, nchunk // 2, body, init)
    pv(nchunk - 1, 1, alpha_last)
    acc = acc_scr[...]
    o = (acc[0:LANES, :] / acc[LANES:LANES + 1, :]).T
    oa, ob = o[0:tq, :], o[tq:2 * tq, :]
    if diff:
        lp = lam_ref[...]
        lam = (jnp.exp(jnp.sum(lp[0:1, :] * lp[1:2, :], axis=1, keepdims=True))
               - jnp.exp(jnp.sum(lp[2:3, :] * lp[3:4, :], axis=1, keepdims=True)) + (1.0 - scale_out))
        o_ref[...] = _rms(oa - lam * ob, nw_ref[...]) * scale_out
    else:
        o_ref[...] = jnp.concatenate([oa, ob], axis=1)


def _flash(diff, qa, qb, k, vt, extra, lambda_init, bsz, seq):
    tq = min(FLASH_TQ, seq)
    ck = min(FLASH_CK, seq // 2)
    assert seq % (2 * ck) == 0 and seq % tq == 0
    nq = seq // tq
    groups = k.shape[1] // LANES
    qa_spec = pl.BlockSpec((tq, LANES), lambda b, g, i: (b * nq + i, g if diff else 2 * g))
    qb_spec = pl.BlockSpec((tq, LANES), lambda b, g, i: (b * nq + i, g if diff else 2 * g + 1))
    k_spec = pl.BlockSpec((seq, LANES), lambda b, g, i: (b, g))
    vt_spec = pl.BlockSpec((1, 1, VT_ROWS, seq), lambda b, g, i: (b, g, 0, 0))
    in_specs = [qa_spec, qb_spec, k_spec, vt_spec]
    args = [qa, qb, k, vt]
    if diff:
        in_specs += [pl.BlockSpec((4, HEAD64), lambda b, g, i: (0, 0)),
                     pl.BlockSpec((1, LANES), lambda b, g, i: (0, 0))]
        args += list(extra)
        out_spec = pl.BlockSpec((tq, LANES), lambda b, g, i: (b * nq + i, g))
    else:
        out_spec = pl.BlockSpec((tq, 2 * LANES), lambda b, g, i: (b * nq + i, g))
    return pl.pallas_call(
        functools.partial(_flash_body, diff, 1.0 - lambda_init, ck),
        grid=(bsz, groups, nq),
        in_specs=in_specs,
        out_specs=out_spec,
        out_shape=jax.ShapeDtypeStruct((bsz * seq, BRANCH), F32),
        scratch_shapes=[pltpu.VMEM((2, ck, 2 * tq), F32), pltpu.VMEM((2, ck, 2 * tq), BF16),
                        pltpu.VMEM((VT_ROWS, 2 * tq), F32)],
        compiler_params=_cparams(("parallel", "parallel", "parallel")),
        name="diff_attn" if diff else "gqa_attn",
    )(*args)


def _outproj_body(x_ref, ya_ref, yb_ref, oc_ref, dg_ref, od_ref, gg_ref, w_ref, pw_ref, o_ref):
    yc = oc_ref[...] * _silu(dg_ref[...])
    yd = od_ref[...] * _silu(gg_ref[...])
    mix = (_dot(ya_ref[...].astype(BF16), w_ref[0]) + _dot(yb_ref[...].astype(BF16), w_ref[1])
           + _dot(yc.astype(BF16), w_ref[2]) + _dot(yd.astype(BF16), w_ref[3]))
    o_ref[...] = x_ref[...] + _rms(mix, pw_ref[...])


def _out_proj(x2, y_a, y_b, o_c, o_d, proj, w4, pw):
    m = x2.shape[0]
    tm = 256
    rows = lambda width, cb: pl.BlockSpec((tm, width), lambda i: (i, cb))
    return pl.pallas_call(
        _outproj_body,
        grid=(m // tm,),
        in_specs=[rows(D_MODEL, 0), rows(BRANCH, 0), rows(BRANCH, 0), rows(BRANCH, 0), rows(BRANCH, C_DG // BRANCH),
                  rows(BRANCH, 0), rows(BRANCH, C_GG // BRANCH),
                  pl.BlockSpec((4, BRANCH, D_MODEL), lambda i: (0, 0, 0)),
                  pl.BlockSpec((1, D_MODEL), lambda i: (0, 0))],
        out_specs=rows(D_MODEL, 0),
        out_shape=jax.ShapeDtypeStruct((m, D_MODEL), F32),
        compiler_params=_cparams(("parallel",)),
        name="out_proj",
    )(x2, y_a, y_b, o_c, proj, o_d, proj, w4, pw)


def _reorder_w_in(w):
    parts = [w[:, 512:1536], w[:, 0:512], w[:, 1552:3088], w[:, 3216:3728], w[:, 3728:5776], w[:, 5776:7312],
             w[:, 3088:3216], w[:, 1536:1552], jnp.zeros((w.shape[0], N_PROJ - 7312), w.dtype)]
    return jnp.concatenate(parts, axis=1).astype(BF16)


def _rope_tables(seq):
    half = HEAD64 // 2
    inv = ROPE_THETA ** (-jnp.arange(half, dtype=F32) / half)
    t = jnp.arange(seq, dtype=jnp.int32)

    def tab(pos):
        ang = pos.astype(F32)[:, None] * inv[None, :]
        cos, sin = jnp.cos(ang), jnp.sin(ang)
        return jnp.concatenate([cos, cos], axis=1), jnp.concatenate([-sin, sin], axis=1)

    cp, sp = tab(t)
    cr, sr = tab(t // GRID_W)
    cc, sc = tab(t % GRID_W)
    return (jnp.concatenate([cp, cp], axis=1), jnp.concatenate([sp, sp], axis=1),
            jnp.concatenate([cr, cc], axis=1), jnp.concatenate([sr, sc], axis=1))


def _pad_lanes(v, width=LANES):
    v = v.reshape(1, -1)
    return jnp.pad(v, ((0, 0), (0, width - v.shape[1])))


def kernel(x, pre_norm_w, post_norm_w, w_in, w_out, conv_w, conv_b, ssm_a_log, ssm_dt_bias, ssm_d, ssm_norm_w,
           rwkv_mu, rwkv_w0, rwkv_w2, rwkv_a0, rwkv_a2, rwkv_k_k, rwkv_k_a, rwkv_r_k, rwkv_ln_w, rwkv_ln_b,
           diff_lambda, diff_norm_w, gqa_q_norm_w, gqa_k_norm_w):
    bsz, seq, _ = x.shape
    depth = w_in.shape[0]
    m = bsz * seq
    x2 = x.reshape(m, D_MODEL)
    tabs = _rope_tables(seq)
    lane_head = jnp.arange(LANES) // HEAD64
    blk = (lane_head[:, None] == lane_head[None, :]).astype(BF16)
    head_of_lane = jnp.arange(BRANCH) // HEAD64
    sels = [(jnp.arange(LANES)[:, None] == (d * SSM_HEADS + head_of_lane)[None, :]).astype(BF16) for d in range(2)]
    zpad = jnp.zeros((RWKV_RANK, BRANCH), F32)
    for i in range(depth):
        lambda_init = 0.8 - 0.6 * math.exp(-0.3 * i)
        proj = _in_proj(x2, pre_norm_w[i].reshape(1, -1), _reorder_w_in(w_in[i]))
        xbc = _conv(proj, jnp.pad(conv_w[i], ((0, SUBLANES - D_CONV), (0, 0))), conv_b[i].reshape(1, -1), bsz, seq)
        dt_bias = _pad_lanes(ssm_dt_bias[i])
        a_log = _pad_lanes(ssm_a_log[i])
        y_f = _ssd(False, xbc, proj, dt_bias, a_log, sels[0], None, bsz, seq)
        d_skip = jnp.repeat(ssm_d[i], HEAD64).reshape(1, -1)
        y_a = _ssd(True, xbc, proj, dt_bias, a_log, sels[1], (y_f, d_skip, ssm_norm_w[i].reshape(1, -1)), bsz, seq)
        prm = {
            "mu_rkv": rwkv_mu[i][:, :3 * BRANCH], "mu_wa": rwkv_mu[i][:, 3 * BRANCH:], "w0": rwkv_w0[i],
            "w2": jnp.concatenate([rwkv_w2[i], jnp.zeros((2, RWKV_RANK, BRANCH), F32)], axis=1),
            "a0": rwkv_a0[i].reshape(1, -1), "a2": jnp.concatenate([zpad, rwkv_a2[i]], axis=0),
            "k_k": rwkv_k_k[i].reshape(1, -1), "k_a": rwkv_k_a[i].reshape(1, -1),
            "r_k": rwkv_r_k[i].reshape(1, -1), "blk": blk,
        }
        ops_f, ops_b, bonus = _rwkv_prep(proj, prm, bsz, seq)
        wkv_f = _wkv(False, ops_f, None, bsz, seq)
        y_b = _wkv(True, ops_b, (wkv_f, bonus, proj, rwkv_ln_w[i].reshape(1, -1), rwkv_ln_b[i].reshape(1, -1), blk),
                   bsz, seq)
        q1, q2, dk, dv, gq, gk, gv = _attn_prep(proj, tabs, gqa_q_norm_w[i].reshape(1, -1),
                                                gqa_k_norm_w[i].reshape(1, -1), bsz, seq)
        o_c = _flash(True, q1, q2, dk, dv, (diff_lambda[i], diff_norm_w[i].reshape(1, -1)), lambda_init, bsz, seq)
        o_d = _flash(False, gq, gq, gk, gv, None, lambda_init, bsz, seq)
        x2 = _out_proj(x2, y_a, y_b, o_c, o_d, proj, w_out[i].reshape(4, BRANCH, D_MODEL).astype(BF16),
                       post_norm_w[i].reshape(1, -1))
    return x2.reshape(bsz, seq, D_MODEL)
```

```python
import functools
import math

import jax
import jax.numpy as jnp
from jax import lax
from jax.experimental import pallas as pl
from jax.experimental.pallas import tpu as pltpu

F32 = jnp.float32
BF16 = jnp.bfloat16

D_MODEL = 1024
GRID_W = 64
ROPE_THETA = 10000.0
NORM_EPS = 1e-6
BRANCH = 512
SSM_HEADS = 8
SSM_STATE = 128
D_CONV = 5
RWKV_RANK = 64
RWKV_GN_EPS = 64e-5
HEAD64 = 64
LANES = 128
SUBLANES = 8

C_XBC, C_Z = 0, 1024
C_RKV, C_RG = 1536, 3072
C_DQ, C_DK, C_DV, C_DG = 3584, 4096, 4608, 5120
C_GQ, C_GK, C_GV, C_GG = 5632, 6144, 6400, 6656
C_WA, C_DT = 7168, 7296
N_PROJ = 7424

SSD_CHUNK = 256
WKV_CHUNK = 64
WKV_SUB = 2
FLASH_TQ = 2048
FLASH_CK = 512
FLASH_UNROLL = 1
VT_ROWS = LANES + 16
LOG2E = 1.4426950408889634
VMEM_LIMIT = 48 * 1024 * 1024


def _cparams(sem):
    return pltpu.CompilerParams(dimension_semantics=sem, vmem_limit_bytes=VMEM_LIMIT)


def _layer_spec(shape, layer):
    return pl.BlockSpec((None,) + tuple(shape), lambda *_: (layer,) + (0,) * len(shape))


def _dot(a, b):
    return jnp.dot(a, b, preferred_element_type=F32)


def _dot_nt(a, b):
    return lax.dot_general(a, b, (((1,), (1,)), ((), ())), preferred_element_type=F32)


def _split2(x):
    hi = x.astype(BF16)
    lo = (x - hi.astype(F32)).astype(BF16)
    return hi, lo


def _split3(x):
    hi = x.astype(BF16)
    r1 = x - hi.astype(F32)
    mid = r1.astype(BF16)
    lo = (r1 - mid.astype(F32)).astype(BF16)
    return hi, mid, lo


def _dot_sel_l(sel, x):
    hi, mid, lo = _split3(x)
    return _dot(sel, hi) + _dot(sel, mid) + _dot(sel, lo)


def _dot_sel_r(x, sel):
    hi, mid, lo = _split3(x)
    return _dot(hi, sel) + _dot(mid, sel) + _dot(lo, sel)


def _dot_sel_r2(x, sel):
    hi, lo = _split2(x)
    return _dot(hi, sel) + _dot(lo, sel)


def _dot_hp(a, b):
    ah, al = _split2(a)
    bh, bl = _split2(b)
    return _dot(ah, bh) + _dot(ah, bl) + _dot(al, bh)


def _dot1(a, b):
    return _dot(a.astype(BF16), b.astype(BF16))


def _dot1_nt(a, b):
    return _dot_nt(a.astype(BF16), b.astype(BF16))


def _softplus(x):
    return jnp.maximum(x, 0.0) + jnp.log(1.0 + jnp.exp(-jnp.abs(x)))


def _silu(x):
    return x * jax.nn.sigmoid(x)


def _rms(x, w):
    return x * lax.rsqrt(jnp.mean(x * x, axis=-1, keepdims=True) + NORM_EPS) * w


def _inproj_body(x_ref, nw_ref, w_ref, o_ref):
    h = _rms(x_ref[...], nw_ref[...]).astype(BF16)
    o_ref[...] = _dot(h, w_ref[...])


def _in_proj(layer, x2, nw, w):
    m = x2.shape[0]
    tm, tn = min(512, m), N_PROJ // 2
    return pl.pallas_call(
        _inproj_body,
        grid=(N_PROJ // tn, m // tm),
        in_specs=[pl.BlockSpec((tm, D_MODEL), lambda j, i: (i, 0)),
                  _layer_spec((1, D_MODEL), layer),
                  pl.BlockSpec((None, D_MODEL, tn), lambda j, i: (layer, 0, j))],
        out_specs=pl.BlockSpec((tm, tn), lambda j, i: (i, j)),
        out_shape=jax.ShapeDtypeStruct((m, N_PROJ), F32),
        compiler_params=_cparams(("parallel", "parallel")),
        name="in_proj",
    )(x2, nw, w)


def _halo_specs(tl, width, col_block, nt, nrows):
    per = tl // SUBLANES
    last = nrows // SUBLANES - 1
    cur = pl.BlockSpec((tl, width), lambda b, i: (b * nt + i, col_block))
    prev = pl.BlockSpec((SUBLANES, width), lambda b, i: (jnp.maximum((b * nt + i) * per - 1, 0), col_block))
    nxt = pl.BlockSpec((SUBLANES, width), lambda b, i: (jnp.minimum((b * nt + i + 1) * per, last), col_block))
    return cur, prev, nxt


def _fill_halo(scr, cur_ref, prev_ref, next_ref):
    i = pl.program_id(1)
    tl = cur_ref.shape[0]
    scr[0:SUBLANES, :] = jnp.where(i > 0, prev_ref[...], 0.0)
    scr[SUBLANES:SUBLANES + tl, :] = cur_ref[...]
    scr[SUBLANES + tl:2 * SUBLANES + tl, :] = jnp.where(i < pl.num_programs(1) - 1, next_ref[...], 0.0)


def _conv_body(cur_ref, prev_ref, next_ref, w_ref, b_ref, o_ref, scr):
    _fill_halo(scr, cur_ref, prev_ref, next_ref)
    tl = cur_ref.shape[0]
    acc = b_ref[...] + w_ref[0:1, :] * scr[pl.ds(SUBLANES - 2, tl), :]
    for j in range(1, D_CONV):
        acc = acc + w_ref[j:j + 1, :] * scr[pl.ds(SUBLANES - 2 + j, tl), :]
    o_ref[...] = _silu(acc)


def _conv(layer, proj, conv_w8, conv_b, bsz, seq):
    tl = min(512, seq)
    nt = seq // tl
    width = 1024
    cur, prev, nxt = _halo_specs(tl, width, C_XBC // width, nt, bsz * seq)
    return pl.pallas_call(
        _conv_body,
        grid=(bsz, nt),
        in_specs=[cur, prev, nxt, _layer_spec((SUBLANES, width), layer), _layer_spec((1, width), layer)],
        out_specs=pl.BlockSpec((tl, width), lambda b, i: (b * nt + i, 0)),
        out_shape=jax.ShapeDtypeStruct((bsz * seq, width), F32),
        scratch_shapes=[pltpu.VMEM((tl + 2 * SUBLANES, width), F32)],
        compiler_params=_cparams(("parallel", "parallel")),
        name="ssm_conv",
    )(proj, proj, proj, conv_w8, conv_b)


def _ssd_body(reverse, *refs):
    if reverse:
        (xs_ref, bc_ref, dt_ref, dtb_ref, alog_ref, e_ref, yf_ref, z_ref, dsk_ref, nw_ref, o_ref, st_ref) = refs
    else:
        (xs_ref, bc_ref, dt_ref, dtb_ref, alog_ref, e_ref, o_ref, st_ref) = refs
    c = pl.program_id(0)

    @pl.when(c == 0)
    def _():
        st_ref[...] = jnp.zeros_like(st_ref)

    nb, t = xs_ref.shape[0], xs_ref.shape[1]
    npair = BRANCH // LANES
    row = lax.broadcasted_iota(jnp.int32, (t, t), 0)
    col = lax.broadcasted_iota(jnp.int32, (t, t), 1)
    tri = (col >= row) if reverse else (col <= row)
    tri_bf = jnp.where(tri, 1.0, 0.0).astype(BF16)
    sel = e_ref[...]
    lane = lax.broadcasted_iota(jnp.int32, (t, LANES), 1)
    first = lane < HEAD64
    ofs = SSM_HEADS if reverse else 0
    pre = []
    for bi in range(nb):
        dt_all = _softplus(dt_ref[bi] + dtb_ref[...])
        a_all = dt_all * (-jnp.exp(alog_ref[...]))
        cum = _dot_sel_l(tri_bf, a_all)
        cum_e = _dot_sel_r(cum, sel)
        xs = xs_ref[bi]
        bc = bc_ref[bi]
        pre.append(dict(cum=cum, cum_t=cum.T, cum_e=cum_e, tot_e=cum_e[0:1, :] if reverse else cum_e[t - 1:t, :],
                        xs=xs, xdt=xs * _dot_sel_r(dt_all, sel),
                        bm=[bc[:, g * SSM_STATE:(g + 1) * SSM_STATE] for g in range(2)],
                        cm=[bc[:, 256 + g * SSM_STATE:256 + (g + 1) * SSM_STATE].astype(BF16) for g in range(2)]))
    cb = {(bi, g): _dot_nt(pre[bi]["cm"][g], pre[bi]["bm"][g].astype(BF16))
          for bi in range(nb) for g in range(2)}
    chains = [(bi, p) for bi in range(nb) for p in range(npair)]
    halves = {}
    for bi, p in chains:
        d = pre[bi]
        xdt_bf = d["xdt"][:, p * LANES:(p + 1) * LANES].astype(BF16)
        for hh in range(2):
            ci = ofs + 2 * p + hh
            dec = jnp.exp(jnp.where(tri, d["cum"][:, ci:ci + 1] - d["cum_t"][ci:ci + 1, :], -jnp.inf))
            halves[bi, p, hh] = _dot((cb[bi, p // 2] * dec).astype(BF16), xdt_bf)
    states = [st_ref[bi * npair + p] for bi, p in chains]
    y_off = [_dot(pre[bi]["cm"][p // 2], s.astype(BF16)) for (bi, p), s in zip(chains, states)]
    loc = []
    for bi, p in chains:
        d = pre[bi]
        sl = slice(p * LANES, (p + 1) * LANES)
        loc.append(_dot(d["bm"][p // 2].T.astype(BF16),
                        (d["xdt"][:, sl] * jnp.exp(d["tot_e"][:, sl] - d["cum_e"][:, sl])).astype(BF16)))
    ys = {}
    for n, (bi, p) in enumerate(chains):
        d = pre[bi]
        sl = slice(p * LANES, (p + 1) * LANES)
        st_ref[bi * npair + p] = states[n] * jnp.exp(d["tot_e"][:, sl]) + loc[n]
        ys[bi, p] = (jnp.where(first, halves[bi, p, 0], halves[bi, p, 1]) + y_off[n] * jnp.exp(d["cum_e"][:, sl]))
    for bi in range(nb):
        y = jnp.concatenate([ys[bi, p] for p in range(npair)], axis=1)
        if not reverse:
            o_ref[bi] = y
            continue
        y = y + yf_ref[bi] + pre[bi]["xs"] * dsk_ref[...]
        yg = y * _silu(z_ref[bi])
        nw = nw_ref[...]
        half = BRANCH // 2
        o_ref[bi] = jnp.concatenate(
            [_rms(yg[:, g * half:(g + 1) * half], nw[:, g * half:(g + 1) * half]) for g in range(2)], axis=1)


def _ssd(layer, reverse, xbc, proj, dt_bias, a_log, sel, extra, bsz, seq):
    t = min(SSD_CHUNK, seq)
    nc = seq // t
    rmap = (lambda c: nc - 1 - c) if reverse else (lambda c: c)
    as3d = lambda a: a.reshape(bsz, seq, a.shape[-1])
    tile = lambda width, cb: pl.BlockSpec((bsz, t, width), lambda c: (0, rmap(c), cb))
    in_specs = [tile(BRANCH, 0), tile(BRANCH, 1), tile(LANES, C_DT // LANES),
                _layer_spec((1, LANES), layer),
                _layer_spec((1, LANES), layer),
                pl.BlockSpec((LANES, BRANCH), lambda c: (0, 0))]
    args = [as3d(xbc), as3d(xbc), as3d(proj), dt_bias, a_log, sel]
    if reverse:
        y_f, d_skip, norm_w = extra
        in_specs += [tile(BRANCH, 0), tile(BRANCH, C_Z // BRANCH),
                     _layer_spec((1, BRANCH), layer),
                     _layer_spec((1, BRANCH), layer)]
        args += [as3d(y_f), as3d(proj), d_skip, norm_w]
    out = pl.pallas_call(
        functools.partial(_ssd_body, reverse),
        grid=(nc,),
        in_specs=in_specs,
        out_specs=tile(BRANCH, 0),
        out_shape=jax.ShapeDtypeStruct((bsz, seq, BRANCH), F32),
        scratch_shapes=[pltpu.VMEM((bsz * (BRANCH // LANES), SSM_STATE, LANES), F32)],
        compiler_params=_cparams(("arbitrary",)),
        name="ssd_bwd" if reverse else "ssd_fwd",
    )(*args)
    return out.reshape(bsz * seq, BRANCH)


def _rwkv_prep_body(cur_ref, prev_ref, next_ref, wcur_ref, wprev_ref, wnext_ref, mu_ref, muw_ref, w0_ref, w2_ref,
                    a0_ref, a2_ref, kk_ref, ka_ref, rk_ref, blk_ref, *outs):
    scr, wscr = outs[-2:]
    bonus_ref = outs[-3]
    o_refs = outs[:-3]
    _fill_halo(scr, cur_ref, prev_ref, next_ref)
    _fill_halo(wscr, wcur_ref, wprev_ref, wnext_ref)
    tl = cur_ref.shape[0]
    u0 = cur_ref[...]
    wa0 = wcur_ref[...]
    blk = blk_ref[...]

    def head_sum(x):
        return jnp.concatenate(
            [_dot_sel_r2(x[:, j * LANES:(j + 1) * LANES], blk) for j in range(BRANCH // LANES)], axis=1)

    bonus = None
    for d in range(2):
        off = SUBLANES - 1 if d == 0 else SUBLANES + 1
        u = u0 + (scr[pl.ds(off, tl), :] - u0) * mu_ref[d:d + 1, :]
        wa = wa0 + (wscr[pl.ds(off, tl), :] - wa0) * muw_ref[d:d + 1, :]
        r, k, v = u[:, 0:BRANCH], u[:, BRANCH:2 * BRANCH], u[:, 2 * BRANCH:3 * BRANCH]
        wlin = w0_ref[d:d + 1, :] + _dot_hp(jnp.tanh(wa), w2_ref[d])
        w = -_softplus(-wlin) - 0.5
        lw = -jnp.exp(w)
        eta = jax.nn.sigmoid(a0_ref[...] + _dot_hp(wa, a2_ref[...]))
        kk = k * kk_ref[...]
        kk = kk / jnp.maximum(jnp.sqrt(head_sum(kk * kk)), 1e-12)
        k2 = k * (1.0 + (eta - 1.0) * ka_ref[...])
        o_refs[6 * d + 0][...] = r
        o_refs[6 * d + 1][...] = k2
        o_refs[6 * d + 2][...] = v
        o_refs[6 * d + 3][...] = lw
        o_refs[6 * d + 4][...] = -kk
        o_refs[6 * d + 5][...] = kk * eta
        bd = head_sum(r * k2 * rk_ref[...]) * v
        bonus = bd if bonus is None else bonus + bd
    bonus_ref[...] = bonus


def _rwkv_prep(layer, proj, prm, bsz, seq):
    tl = min(256, seq)
    nt = seq // tl
    m = bsz * seq
    wide = 3 * BRANCH
    cur, prev, nxt = _halo_specs(tl, wide, C_RKV // wide, nt, m)
    wcur, wprev, wnxt = _halo_specs(tl, LANES, C_WA // LANES, nt, m)
    full = lambda shape: _layer_spec(shape, layer)
    out_spec = pl.BlockSpec((tl, BRANCH), lambda b, i: (b * nt + i, 0))
    outs = pl.pallas_call(
        _rwkv_prep_body,
        grid=(bsz, nt),
        in_specs=[cur, prev, nxt, wcur, wprev, wnxt,
                  full((2, wide)), full((2, LANES)), full((2, BRANCH)), full((2, LANES, BRANCH)),
                  full((1, BRANCH)), full((LANES, BRANCH)), full((1, BRANCH)), full((1, BRANCH)),
                  full((1, BRANCH)), pl.BlockSpec((LANES, LANES), lambda b, i: (0, 0))],
        out_specs=[out_spec] * 13,
        out_shape=[jax.ShapeDtypeStruct((m, BRANCH), F32)] * 13,
        scratch_shapes=[pltpu.VMEM((tl + 2 * SUBLANES, wide), F32), pltpu.VMEM((tl + 2 * SUBLANES, LANES), F32)],
        compiler_params=_cparams(("parallel", "parallel")),
        name="rwkv_prep",
    )(proj, proj, proj, proj, proj, proj, prm["mu_rkv"], prm["mu_wa"], prm["w0"], prm["w2"], prm["a0"], prm["a2"],
      prm["k_k"], prm["k_a"], prm["r_k"], prm["blk"])
    return outs[0:6], outs[6:12], outs[12]


def _wkv_body(reverse, *refs):
    if reverse:
        (r_ref, k_ref, v_ref, lw_ref, a_ref, b_ref, yf_ref, bonus_ref, g_ref, lnw_ref, lnb_ref, blk_ref,
         o_ref, s_ref) = refs
    else:
        (r_ref, k_ref, v_ref, lw_ref, a_ref, b_ref, o_ref, s_ref) = refs
    ci = pl.program_id(0)

    @pl.when(ci == 0)
    def _():
        s_ref[...] = jnp.zeros_like(s_ref)

    nb, rows = r_ref.shape[0], r_ref.shape[1]
    c = min(WKV_CHUNK, rows)
    nsub = rows // c
    c2 = 2 * c
    npair = BRANCH // LANES
    row = lax.broadcasted_iota(jnp.int32, (c, c), 0)
    col = lax.broadcasted_iota(jnp.int32, (c, c), 1)
    tri_bf = jnp.where((col >= row) if reverse else (col <= row), 1.0, 0.0).astype(BF16)
    lane = lax.broadcasted_iota(jnp.int32, (c, LANES), 1)
    first = lane < HEAD64
    row2 = lax.broadcasted_iota(jnp.int32, (c2, c2), 0)
    col2 = lax.broadcasted_iota(jnp.int32, (c2, c2), 1)
    same = jnp.where(row2 >= c, 1, 0) == jnp.where(col2 >= c, 1, 0)
    if reverse:
        strict, incl = same & (col2 > row2), same & (col2 >= row2)
    else:
        strict, incl = same & (col2 < row2), same & (col2 <= row2)
    diag = row2 == col2
    eye2 = jnp.where(diag, 1.0, 0.0)
    zero_blk = jnp.zeros((c2, LANES), F32)
    squarings = max(1, int(math.ceil(math.log2(c))) - 1)

    def stack(x):
        return jnp.concatenate([jnp.where(first, x, 0.0), jnp.where(first, 0.0, x)], axis=0)

    subs = list(range(nsub))[::-1] if reverse else list(range(nsub))
    chains = [(bi, sub, p) for sub in subs for bi in range(nb) for p in range(npair)]
    pre = {}
    for bi, sub in [(bi, sub) for sub in subs for bi in range(nb)]:
        rs = slice(sub * c, (sub + 1) * c)
        r, k, v, lw, a, b = (ref[bi, rs, :] for ref in (r_ref, k_ref, v_ref, lw_ref, a_ref, b_ref))
        cum = _dot_sel_l(tri_bf, lw)
        cum_prev = cum - lw
        tot = cum[0:1, :] if reverse else cum[c - 1:c, :]
        mid = cum[c // 2:c // 2 + 1, :]
        e_in = jnp.exp(mid - cum)
        e_end = jnp.exp(tot - cum)
        pre[bi, sub] = dict(am=a * jnp.exp(cum_prev - mid), bm=b * e_in, km=k * e_in, rm=r * jnp.exp(cum - mid),
                            a0=a * jnp.exp(cum_prev), r0=r * jnp.exp(cum), bend=b * e_end, kend=k * e_end,
                            gdiag=jnp.exp(tot), v=v)

    def part(name, ch):
        bi, sub, p = ch
        return pre[bi, sub][name][:, p * LANES:(p + 1) * LANES]

    v_s = [stack(part("v", ch)) for ch in chains]
    prod = [_dot_nt(jnp.concatenate([stack(part("am", ch)), stack(part("rm", ch))], axis=0).astype(BF16),
                    jnp.concatenate([stack(part("bm", ch)), stack(part("km", ch))], axis=0).astype(BF16))
            for ch in chains]
    l_ab = [jnp.where(strict, x[0:c2, 0:c2], 0.0) for x in prod]
    l_ak = [jnp.where(strict, x[0:c2, c2:2 * c2], 0.0) for x in prod]
    a_rb = [jnp.where(incl, x[c2:2 * c2, 0:c2], 0.0) for x in prod]
    a_rk = [jnp.where(incl, x[c2:2 * c2, c2:2 * c2], 0.0) for x in prod]
    tinv = [eye2 + x for x in l_ab]
    pw = [x.astype(BF16) for x in l_ab]
    pw = [_dot(x, x).astype(BF16) for x in pw]
    lakv = [_dot1(x, y) for x, y in zip(l_ak, v_s)]
    for i in range(squarings):
        if i == squarings - 1:
            tinv = [t + _dot(t.astype(BF16), x) for t, x in zip(tinv, pw)]
        else:
            xs = [_dot(jnp.concatenate([t.astype(BF16), x], axis=0), x) for t, x in zip(tinv, pw)]
            tinv = [t + x[0:c2, :] for t, x in zip(tinv, xs)]
            pw = [x[c2:2 * c2, :].astype(BF16) for x in xs]
    au = [_dot(t.astype(BF16), jnp.concatenate([stack(part("a0", ch)), x], axis=1).astype(BF16))
          for t, x, ch in zip(tinv, lakv, chains)]
    big = []
    for n, ch in enumerate(chains):
        ends_t = jnp.concatenate([stack(part("bend", ch)), stack(part("kend", ch))], axis=0).T
        lhs3 = jnp.concatenate([jnp.concatenate([a_rb[n], a_rk[n]], axis=1), ends_t], axis=0).astype(BF16)
        rhs3 = jnp.concatenate([au[n], jnp.concatenate([zero_blk, v_s[n]], axis=1)], axis=0).astype(BF16)
        big.append(_dot(lhs3, rhs3))
    lhs4 = []
    for n, ch in enumerate(chains):
        r_hat = stack(part("r0", ch)) + big[n][0:c2, 0:LANES]
        gmat = jnp.where(diag, part("gdiag", ch), 0.0) + big[n][c2:c2 + LANES, 0:LANES]
        g_hi, g_lo = _split2(gmat)
        lhs4.append((jnp.concatenate([r_hat.astype(BF16), g_hi], axis=0), g_lo))
    ys = {}
    per_sub = nb * npair
    for si, sub in enumerate(subs):
        idx = range(si * per_sub, (si + 1) * per_sub)
        st = [_split2(s_ref[n - si * per_sub]) for n in idx]
        res = [_dot(lhs4[n][0], s_hi) for n, (s_hi, s_lo) in zip(idx, st)]
        fine = [_dot(lhs4[n][0][c2:c2 + LANES, :], s_lo) + _dot(lhs4[n][1], s_hi) for n, (s_hi, s_lo) in zip(idx, st)]
        for n, x, f in zip(idx, res, fine):
            s_ref[n - si * per_sub] = x[c2:c2 + LANES, :] + f + big[n][c2:c2 + LANES, LANES:2 * LANES]
            y_s = x[0:c2, :] + big[n][0:c2, LANES:2 * LANES]
            ys[chains[n]] = y_s[0:c, :] + y_s[c:c2, :]
    for bi, sub in [(bi, sub) for sub in subs for bi in range(nb)]:
        rs = slice(sub * c, (sub + 1) * c)
        y = jnp.concatenate([ys[bi, sub, p] for p in range(npair)], axis=1)
        if not reverse:
            o_ref[bi, rs, :] = y
            continue
        wkv = y + yf_ref[bi, rs, :]
        blk = blk_ref[...]
        inv_n = 1.0 / HEAD64

        def head_mean(x):
            return jnp.concatenate(
                [_dot_sel_r2(x[:, j * LANES:(j + 1) * LANES], blk) for j in range(npair)], axis=1) * inv_n

        cen = wkv - head_mean(wkv)
        gn = cen * lax.rsqrt(head_mean(cen * cen) + RWKV_GN_EPS) * lnw_ref[...] + lnb_ref[...]
        o_ref[bi, rs, :] = (gn + bonus_ref[bi, rs, :]) * _silu(g_ref[bi, rs, :])


def _wkv(layer, reverse, ops, extra, bsz, seq):
    c = min(WKV_CHUNK * WKV_SUB, seq)
    nc = seq // c
    npair = BRANCH // LANES
    rmap = (lambda ci: nc - 1 - ci) if reverse else (lambda ci: ci)
    as3d = lambda t: t.reshape(bsz, seq, t.shape[-1])
    tile = pl.BlockSpec((bsz, c, BRANCH), lambda ci: (0, rmap(ci), 0))
    in_specs = [tile] * 6
    args = [as3d(t) for t in ops]
    if reverse:
        y_f, bonus, proj, ln_w, ln_b, blk = extra
        in_specs += [tile, tile,
                     pl.BlockSpec((bsz, c, BRANCH), lambda ci: (0, rmap(ci), C_RG // BRANCH)),
                     _layer_spec((1, BRANCH), layer),
                     _layer_spec((1, BRANCH), layer),
                     pl.BlockSpec((LANES, LANES), lambda ci: (0, 0))]
        args += [as3d(y_f), as3d(bonus), as3d(proj), ln_w, ln_b, blk]
    out = pl.pallas_call(
        functools.partial(_wkv_body, reverse),
        grid=(nc,),
        in_specs=in_specs,
        out_specs=tile,
        out_shape=jax.ShapeDtypeStruct((bsz, seq, BRANCH), F32),
        scratch_shapes=[pltpu.VMEM((bsz * npair, LANES, LANES), F32)],
        compiler_params=_cparams(("arbitrary",)),
        name="wkv_bwd" if reverse else "wkv_fwd",
    )(*args)
    return out.reshape(bsz * seq, BRANCH)


def _rope(x, cos, sin_signed):
    width = x.shape[1]
    reps = width // LANES
    lane = lax.broadcasted_iota(jnp.int32, x.shape, 1)
    first_half = (lane & (HEAD64 - 1)) < (HEAD64 // 2)
    partner = jnp.where(first_half, pltpu.roll(x, width - HEAD64 // 2, 1), pltpu.roll(x, HEAD64 // 2, 1))
    cos_w = jnp.concatenate([cos] * reps, axis=1) if reps > 1 else cos
    sin_w = jnp.concatenate([sin_signed] * reps, axis=1) if reps > 1 else sin_signed
    return x * cos_w + partner * sin_w


def _attn_prep_body(dq_ref, dk_ref, dv_ref, gq_ref, gk_ref, gv_ref, cd_ref, sd_ref, cg_ref, sg_ref, qw_ref, kw_ref,
                    q1_ref, q2_ref, dko_ref, dvo_ref, gqo_ref, gko_ref, gvo_ref):
    cd, sd, cg, sg = cd_ref[...], sd_ref[...], cg_ref[...], sg_ref[...]
    q = _rope(dq_ref[...], cd, sd) * (HEAD64 ** -0.5 * LOG2E)
    lane = lax.broadcasted_iota(jnp.int32, q.shape, 1)
    comp0 = (lane & (LANES - 1)) < HEAD64
    q1_ref[...] = jnp.where(comp0, q, 0.0).astype(BF16)
    q2_ref[...] = jnp.where(comp0, 0.0, q).astype(BF16)
    dko_ref[...] = _rope(dk_ref[...], cd, sd).astype(BF16)

    def store_vt(v_ref, o_ref):
        vt = v_ref[...].T
        tl = vt.shape[1]
        for h in range(vt.shape[0] // LANES):
            o_ref[0, h, 0:LANES, :] = vt[h * LANES:(h + 1) * LANES, :].astype(BF16)
            o_ref[0, h, LANES:VT_ROWS, :] = jnp.ones((VT_ROWS - LANES, tl), BF16)

    store_vt(dv_ref, dvo_ref)

    def norm_heads(x, w):
        return jnp.concatenate(
            [_rms(x[:, j * LANES:(j + 1) * LANES], w) for j in range(x.shape[1] // LANES)], axis=1)

    gq = _rope(norm_heads(gq_ref[...], qw_ref[...]), cg, sg) * (LANES ** -0.5 * LOG2E)
    gqo_ref[...] = gq.astype(BF16)
    gko_ref[...] = _rope(norm_heads(gk_ref[...], kw_ref[...]), cg, sg).astype(BF16)
    store_vt(gv_ref, gvo_ref)


def _attn_prep(layer, proj, tabs, qw, kw, bsz, seq):
    tl = min(256, seq)
    nt = seq // tl
    m = bsz * seq
    colspec = lambda width, off: pl.BlockSpec((tl, width), lambda b, i: (b * nt + i, off // width))
    tab = pl.BlockSpec((tl, LANES), lambda b, i: (i, 0))
    wspec = _layer_spec((1, LANES), layer)
    o512 = pl.BlockSpec((tl, BRANCH), lambda b, i: (b * nt + i, 0))
    o256 = pl.BlockSpec((tl, 256), lambda b, i: (b * nt + i, 0))
    s512 = jax.ShapeDtypeStruct((m, BRANCH), BF16)
    s256 = jax.ShapeDtypeStruct((m, 256), BF16)
    vt_spec = lambda heads: pl.BlockSpec((1, heads, VT_ROWS, tl), lambda b, i: (b, 0, 0, i))
    vt_shape = lambda heads: jax.ShapeDtypeStruct((bsz, heads, VT_ROWS, seq), BF16)
    return pl.pallas_call(
        _attn_prep_body,
        grid=(bsz, nt),
        in_specs=[colspec(BRANCH, C_DQ), colspec(BRANCH, C_DK), colspec(BRANCH, C_DV), colspec(BRANCH, C_GQ),
                  colspec(256, C_GK), colspec(256, C_GV), tab, tab, tab, tab, wspec, wspec],
        out_specs=[o512, o512, o512, vt_spec(4), o512, o256, vt_spec(2)],
        out_shape=[s512, s512, s512, vt_shape(4), s512, s256, vt_shape(2)],
        compiler_params=_cparams(("parallel", "parallel")),
        name="attn_prep",
    )(proj, proj, proj, proj, proj, proj, *tabs, qw, kw)


def _flash_body(diff, scale_out, ck, *refs):
    if diff:
        qa_ref, qb_ref, k_ref, vt_ref, lam_ref, nw_ref, o_ref, s_scr, p_scr, acc_scr = refs
    else:
        qa_ref, qb_ref, k_ref, vt_ref, o_ref, s_scr, p_scr, acc_scr = refs
    tq = qa_ref.shape[0]
    q = jnp.concatenate([qa_ref[...], qb_ref[...]], axis=0)
    nchunk = k_ref.shape[0] // ck

    def scores(j, slot):
        off = pl.multiple_of(j * ck, ck)
        s = _dot_nt(k_ref[pl.ds(off, ck), :], q)
        s_scr[slot] = s
        return jnp.max(s, axis=0, keepdims=True)

    def pv(j, slot, alpha):
        off = pl.multiple_of(j * ck, ck)
        acc_scr[...] = alpha * acc_scr[...] + _dot(vt_ref[0, 0, :, pl.ds(off, ck)], p_scr[slot])

    def step(j, slot, m_prev, cmax, alpha_prev):
        cmax_next = scores(jnp.minimum(j + 1, nchunk - 1), 1 - slot)
        pv(jnp.maximum(j - 1, 0), 1 - slot, alpha_prev)
        m_new = jnp.maximum(m_prev, cmax)
        p_scr[slot] = jnp.exp2(s_scr[slot] - m_new).astype(BF16)
        return m_new, cmax_next, jnp.exp2(m_prev - m_new)

    def body(i, carry):
        m, cmax, alpha = carry
        m, cmax, alpha = step(2 * i, 0, m, cmax, alpha)
        return step(2 * i + 1, 1, m, cmax, alpha)

    acc_scr[...] = jnp.zeros_like(acc_scr)
    p_scr[1] = jnp.zeros(p_scr.shape[1:], BF16)
    init = (jnp.full((1, 2 * tq), -jnp.inf, F32), scores(0, 0), jnp.ones((1, 2 * tq), F32))
    _, _, alpha_last = lax.fori_loop(0, nchunk // 2, body, init)
    pv(nchunk - 1, 1, alpha_last)
    acc = acc_scr[...]
    o = (acc[0:LANES, :] / acc[LANES:LANES + 1, :]).T
    oa, ob = o[0:tq, :], o[tq:2 * tq, :]
    if diff:
        lp = lam_ref[...]
        lam = (jnp.exp(jnp.sum(lp[0:1, :] * lp[1:2, :], axis=1, keepdims=True))
               - jnp.exp(jnp.sum(lp[2:3, :] * lp[3:4, :], axis=1, keepdims=True)) + (1.0 - scale_out))
        o_ref[...] = _rms(oa - lam * ob, nw_ref[...]) * scale_out
    else:
        o_ref[...] = jnp.concatenate([oa, ob], axis=1)


def _flash(layer, diff, qa, qb, k, vt, extra, lambda_init, bsz, seq):
    tq = min(FLASH_TQ, seq)
    ck = min(FLASH_CK, seq // 2)
    assert seq % (2 * ck) == 0 and seq % tq == 0
    nq = seq // tq
    groups = k.shape[1] // LANES
    qa_spec = pl.BlockSpec((tq, LANES), lambda b, g, i: (b * nq + i, g if diff else 2 * g))
    qb_spec = pl.BlockSpec((tq, LANES), lambda b, g, i: (b * nq + i, g if diff else 2 * g + 1))
    k_spec = pl.BlockSpec((seq, LANES), lambda b, g, i: (b, g))
    vt_spec = pl.BlockSpec((1, 1, VT_ROWS, seq), lambda b, g, i: (b, g, 0, 0))
    in_specs = [qa_spec, qb_spec, k_spec, vt_spec]
    args = [qa, qb, k, vt]
    if diff:
        in_specs += [_layer_spec((4, HEAD64), layer), _layer_spec((1, LANES), layer)]
        args += list(extra)
        out_spec = pl.BlockSpec((tq, LANES), lambda b, g, i: (b * nq + i, g))
    else:
        out_spec = pl.BlockSpec((tq, 2 * LANES), lambda b, g, i: (b * nq + i, g))
    return pl.pallas_call(
        functools.partial(_flash_body, diff, 1.0 - lambda_init, ck),
        grid=(bsz, groups, nq),
        in_specs=in_specs,
        out_specs=out_spec,
        out_shape=jax.ShapeDtypeStruct((bsz * seq, BRANCH), F32),
        scratch_shapes=[pltpu.VMEM((2, ck, 2 * tq), F32), pltpu.VMEM((2, ck, 2 * tq), BF16),
                        pltpu.VMEM((VT_ROWS, 2 * tq), F32)],
        compiler_params=_cparams(("parallel", "parallel", "parallel")),
        name="diff_attn" if diff else "gqa_attn",
    )(*args)


def _outproj_body(x_ref, ya_ref, yb_ref, oc_ref, dg_ref, od_ref, gg_ref, w_ref, pw_ref, o_ref):
    yc = oc_ref[...] * _silu(dg_ref[...])
    yd = od_ref[...] * _silu(gg_ref[...])
    mix = (_dot(ya_ref[...].astype(BF16), w_ref[0]) + _dot(yb_ref[...].astype(BF16), w_ref[1])
           + _dot(yc.astype(BF16), w_ref[2]) + _dot(yd.astype(BF16), w_ref[3]))
    o_ref[...] = x_ref[...] + _rms(mix, pw_ref[...])


def _out_proj(layer, x2, y_a, y_b, o_c, o_d, proj, w4, pw):
    m = x2.shape[0]
    tm = 256
    rows = lambda width, cb: pl.BlockSpec((tm, width), lambda i: (i, cb))
    return pl.pallas_call(
        _outproj_body,
        grid=(m // tm,),
        in_specs=[rows(D_MODEL, 0), rows(BRANCH, 0), rows(BRANCH, 0), rows(BRANCH, 0), rows(BRANCH, C_DG // BRANCH),
                  rows(BRANCH, 0), rows(BRANCH, C_GG // BRANCH),
                  _layer_spec((4, BRANCH, D_MODEL), layer), _layer_spec((1, D_MODEL), layer)],
        out_specs=rows(D_MODEL, 0),
        out_shape=jax.ShapeDtypeStruct((m, D_MODEL), F32),
        compiler_params=_cparams(("parallel",)),
        name="out_proj",
    )(x2, y_a, y_b, o_c, proj, o_d, proj, w4, pw)


def _reorder_w_in(w):
    w = w.astype(BF16)
    parts = [w[..., 512:1536], w[..., 0:512], w[..., 1552:3088], w[..., 3216:3728], w[..., 3728:5776],
             w[..., 5776:7312], w[..., 3088:3216], w[..., 1536:1552],
             jnp.zeros(w.shape[:-1] + (N_PROJ - 7312,), w.dtype)]
    return jnp.concatenate(parts, axis=-1)


def _rope_tables(seq):
    half = HEAD64 // 2
    inv = ROPE_THETA ** (-jnp.arange(half, dtype=F32) / half)
    t = jnp.arange(seq, dtype=jnp.int32)

    def tab(pos):
        ang = pos.astype(F32)[:, None] * inv[None, :]
        cos, sin = jnp.cos(ang), jnp.sin(ang)
        return jnp.concatenate([cos, cos], axis=1), jnp.concatenate([-sin, sin], axis=1)

    cp, sp = tab(t)
    cr, sr = tab(t // GRID_W)
    cc, sc = tab(t % GRID_W)
    return (jnp.concatenate([cp, cp], axis=1), jnp.concatenate([sp, sp], axis=1),
            jnp.concatenate([cr, cc], axis=1), jnp.concatenate([sr, sc], axis=1))


def _pad_lanes(v, width=LANES):
    v = v.reshape(v.shape[0], 1, -1)
    return jnp.pad(v, ((0, 0), (0, 0), (0, width - v.shape[2])))


def kernel(x, pre_norm_w, post_norm_w, w_in, w_out, conv_w, conv_b, ssm_a_log, ssm_dt_bias, ssm_d, ssm_norm_w,
           rwkv_mu, rwkv_w0, rwkv_w2, rwkv_a0, rwkv_a2, rwkv_k_k, rwkv_k_a, rwkv_r_k, rwkv_ln_w, rwkv_ln_b,
           diff_lambda, diff_norm_w, gqa_q_norm_w, gqa_k_norm_w):
    bsz, seq, _ = x.shape
    depth = w_in.shape[0]
    m = bsz * seq
    x2 = x.reshape(m, D_MODEL)
    tabs = _rope_tables(seq)
    lane_head = jnp.arange(LANES) // HEAD64
    blk = (lane_head[:, None] == lane_head[None, :]).astype(BF16)
    head_of_lane = jnp.arange(BRANCH) // HEAD64
    sels = [(jnp.arange(LANES)[:, None] == (d * SSM_HEADS + head_of_lane)[None, :]).astype(BF16) for d in range(2)]
    row = lambda v: v.reshape(depth, 1, -1)
    w_in_r = _reorder_w_in(w_in)
    w_out4 = w_out.reshape(depth, 4, BRANCH, D_MODEL).astype(BF16)
    pre_w, post_w = row(pre_norm_w), row(post_norm_w)
    conv_w8 = jnp.pad(conv_w, ((0, 0), (0, SUBLANES - D_CONV), (0, 0)))
    conv_b1 = row(conv_b)
    dt_bias, a_log = _pad_lanes(ssm_dt_bias), _pad_lanes(ssm_a_log)
    d_skip = row(jnp.repeat(ssm_d, HEAD64, axis=-1))
    ssm_nw = row(ssm_norm_w)
    prm = {
        "mu_rkv": rwkv_mu[:, :, :3 * BRANCH], "mu_wa": rwkv_mu[:, :, 3 * BRANCH:], "w0": rwkv_w0,
        "w2": jnp.pad(rwkv_w2, ((0, 0), (0, 0), (0, LANES - RWKV_RANK), (0, 0))),
        "a0": row(rwkv_a0), "a2": jnp.pad(rwkv_a2, ((0, 0), (LANES - RWKV_RANK, 0), (0, 0))),
        "k_k": row(rwkv_k_k), "k_a": row(rwkv_k_a), "r_k": row(rwkv_r_k), "blk": blk,
    }
    ln_w, ln_b = row(rwkv_ln_w), row(rwkv_ln_b)
    diff_nw, gqa_qw, gqa_kw = row(diff_norm_w), row(gqa_q_norm_w), row(gqa_k_norm_w)
    for i in range(depth):
        lambda_init = 0.8 - 0.6 * math.exp(-0.3 * i)
        proj = _in_proj(i, x2, pre_w, w_in_r)
        xbc = _conv(i, proj, conv_w8, conv_b1, bsz, seq)
        y_f = _ssd(i, False, xbc, proj, dt_bias, a_log, sels[0], None, bsz, seq)
        y_a = _ssd(i, True, xbc, proj, dt_bias, a_log, sels[1], (y_f, d_skip, ssm_nw), bsz, seq)
        ops_f, ops_b, bonus = _rwkv_prep(i, proj, prm, bsz, seq)
        wkv_f = _wkv(i, False, ops_f, None, bsz, seq)
        y_b = _wkv(i, True, ops_b, (wkv_f, bonus, proj, ln_w, ln_b, blk), bsz, seq)
        q1, q2, dk, dv, gq, gk, gv = _attn_prep(i, proj, tabs, gqa_qw, gqa_kw, bsz, seq)
        o_c = _flash(i, True, q1, q2, dk, dv, (diff_lambda, diff_nw), lambda_init, bsz, seq)
        o_d = _flash(i, False, gq, gq, gk, gv, None, lambda_init, bsz, seq)
        x2 = _out_proj(i, x2, y_a, y_b, o_c, o_d, proj, w_out4, post_w)
    return x2.reshape(bsz, seq, D_MODEL)
```

```python
import functools
import math

import jax
import jax.numpy as jnp
import numpy as np
from jax import lax
from jax.experimental import pallas as pl
from jax.experimental.pallas import tpu as pltpu

F32 = jnp.float32
BF16 = jnp.bfloat16

D_MODEL = 1024
GRID_W = 64
ROPE_THETA = 10000.0
NORM_EPS = 1e-6
BRANCH = 512
SSM_HEADS = 8
SSM_STATE = 128
D_CONV = 5
RWKV_RANK = 64
RWKV_GN_EPS = 64e-5
HEAD64 = 64
LANES = 128
SUBLANES = 8

C_XBC, C_Z = 0, 1024
C_RKV, C_RG = 1536, 3072
C_DQ, C_DK, C_DV, C_DG = 3584, 4096, 4608, 5120
C_GQ, C_GK, C_GV, C_GG = 5632, 6144, 6400, 6656
C_WA, C_DT = 7168, 7296
N_PROJ = 7424

SSD_CHUNK = 256
WKV_CHUNK = 64
WKV_SUB = 2
FLASH_TQ = 2048
FLASH_CK = 512
FLASH_UNROLL = 1
VT_ROWS = LANES + 16
LOG2E = 1.4426950408889634
VMEM_LIMIT = 48 * 1024 * 1024


def _cparams(sem):
    return pltpu.CompilerParams(dimension_semantics=sem, vmem_limit_bytes=VMEM_LIMIT)


def _layer_spec(shape, layer):
    return pl.BlockSpec((None,) + tuple(shape), lambda *_: (layer,) + (0,) * len(shape))


def _dot(a, b):
    return jnp.dot(a, b, preferred_element_type=F32)


def _dot_nt(a, b):
    return lax.dot_general(a, b, (((1,), (1,)), ((), ())), preferred_element_type=F32)


def _split2(x):
    hi = x.astype(BF16)
    lo = (x - hi.astype(F32)).astype(BF16)
    return hi, lo


def _split3(x):
    hi = x.astype(BF16)
    r1 = x - hi.astype(F32)
    mid = r1.astype(BF16)
    lo = (r1 - mid.astype(F32)).astype(BF16)
    return hi, mid, lo


def _dot_sel_l(sel, x):
    hi, mid, lo = _split3(x)
    return _dot(sel, hi) + _dot(sel, mid) + _dot(sel, lo)


def _dot_sel_r(x, sel):
    hi, mid, lo = _split3(x)
    return _dot(hi, sel) + _dot(mid, sel) + _dot(lo, sel)


def _dot_sel_r2(x, sel):
    hi, lo = _split2(x)
    return _dot(hi, sel) + _dot(lo, sel)


def _dot_hp(a, b):
    ah, al = _split2(a)
    bh, bl = _split2(b)
    return _dot(ah, bh) + _dot(ah, bl) + _dot(al, bh)


def _dot1(a, b):
    return _dot(a.astype(BF16), b.astype(BF16))


def _dot1_nt(a, b):
    return _dot_nt(a.astype(BF16), b.astype(BF16))


def _softplus(x):
    return jnp.maximum(x, 0.0) + jnp.log(1.0 + jnp.exp(-jnp.abs(x)))


def _silu(x):
    return x * jax.nn.sigmoid(x)


def _rms(x, w):
    return x * lax.rsqrt(jnp.mean(x * x, axis=-1, keepdims=True) + NORM_EPS) * w


def _inproj_body(x_ref, nw_ref, w_ref, o_ref):
    h = _rms(x_ref[...], nw_ref[...]).astype(BF16)
    o_ref[...] = _dot_nt(h, w_ref[...])


def _in_proj(layer, x2, nw, w):
    m = x2.shape[0]
    tm, tn = min(512, m), N_PROJ // 2
    return pl.pallas_call(
        _inproj_body,
        grid=(N_PROJ // tn, m // tm),
        in_specs=[pl.BlockSpec((tm, D_MODEL), lambda j, i: (i, 0)),
                  _layer_spec((1, D_MODEL), layer),
                  pl.BlockSpec((None, tn, D_MODEL), lambda j, i: (layer, j, 0))],
        out_specs=pl.BlockSpec((tm, tn), lambda j, i: (i, j)),
        out_shape=jax.ShapeDtypeStruct((m, N_PROJ), F32),
        compiler_params=_cparams(("parallel", "parallel")),
        name="in_proj",
    )(x2, nw, w)


def _halo_specs(tl, width, col_block, nt, nrows):
    per = tl // SUBLANES
    last = nrows // SUBLANES - 1
    cur = pl.BlockSpec((tl, width), lambda b, i: (b * nt + i, col_block))
    prev = pl.BlockSpec((SUBLANES, width), lambda b, i: (jnp.maximum((b * nt + i) * per - 1, 0), col_block))
    nxt = pl.BlockSpec((SUBLANES, width), lambda b, i: (jnp.minimum((b * nt + i + 1) * per, last), col_block))
    return cur, prev, nxt


def _fill_halo(scr, cur_ref, prev_ref, next_ref):
    i = pl.program_id(1)
    tl = cur_ref.shape[0]
    scr[0:SUBLANES, :] = jnp.where(i > 0, prev_ref[...], 0.0)
    scr[SUBLANES:SUBLANES + tl, :] = cur_ref[...]
    scr[SUBLANES + tl:2 * SUBLANES + tl, :] = jnp.where(i < pl.num_programs(1) - 1, next_ref[...], 0.0)


def _conv_body(cur_ref, prev_ref, next_ref, w_ref, b_ref, o_ref, scr):
    _fill_halo(scr, cur_ref, prev_ref, next_ref)
    tl = cur_ref.shape[0]
    acc = b_ref[...] + w_ref[0:1, :] * scr[pl.ds(SUBLANES - 2, tl), :]
    for j in range(1, D_CONV):
        acc = acc + w_ref[j:j + 1, :] * scr[pl.ds(SUBLANES - 2 + j, tl), :]
    o_ref[...] = _silu(acc)


def _conv(layer, proj, conv_w8, conv_b, bsz, seq):
    tl = min(512, seq)
    nt = seq // tl
    width = 1024
    cur, prev, nxt = _halo_specs(tl, width, C_XBC // width, nt, bsz * seq)
    return pl.pallas_call(
        _conv_body,
        grid=(bsz, nt),
        in_specs=[cur, prev, nxt, _layer_spec((SUBLANES, width), layer), _layer_spec((1, width), layer)],
        out_specs=pl.BlockSpec((tl, width), lambda b, i: (b * nt + i, 0)),
        out_shape=jax.ShapeDtypeStruct((bsz * seq, width), F32),
        scratch_shapes=[pltpu.VMEM((tl + 2 * SUBLANES, width), F32)],
        compiler_params=_cparams(("parallel", "parallel")),
        name="ssm_conv",
    )(proj, proj, proj, conv_w8, conv_b)


def _ssd_body(reverse, *refs):
    if reverse:
        (xs_ref, bc_ref, dt_ref, dtb_ref, alog_ref, e_ref, yf_ref, z_ref, dsk_ref, nw_ref, o_ref, st_ref) = refs
    else:
        (xs_ref, bc_ref, dt_ref, dtb_ref, alog_ref, e_ref, o_ref, st_ref) = refs
    c = pl.program_id(0)

    @pl.when(c == 0)
    def _():
        st_ref[...] = jnp.zeros_like(st_ref)

    nb, t = xs_ref.shape[0], xs_ref.shape[1]
    npair = BRANCH // LANES
    row = lax.broadcasted_iota(jnp.int32, (t, t), 0)
    col = lax.broadcasted_iota(jnp.int32, (t, t), 1)
    tri = (col >= row) if reverse else (col <= row)
    tri_bf = jnp.where(tri, 1.0, 0.0).astype(BF16)
    sel = e_ref[...]
    lane = lax.broadcasted_iota(jnp.int32, (t, LANES), 1)
    first = lane < HEAD64
    ofs = SSM_HEADS if reverse else 0
    pre = []
    for bi in range(nb):
        dt_all = _softplus(dt_ref[bi] + dtb_ref[...])
        a_all = dt_all * (-jnp.exp(alog_ref[...]))
        cum = _dot_sel_l(tri_bf, a_all)
        cum_e = _dot_sel_r(cum, sel)
        xs = xs_ref[bi]
        bc = bc_ref[bi]
        pre.append(dict(cum=cum, cum_t=cum.T, cum_e=cum_e, tot_e=cum_e[0:1, :] if reverse else cum_e[t - 1:t, :],
                        xs=xs, xdt=xs * _dot_sel_r(dt_all, sel),
                        bm=[bc[:, g * SSM_STATE:(g + 1) * SSM_STATE] for g in range(2)],
                        cm=[bc[:, 256 + g * SSM_STATE:256 + (g + 1) * SSM_STATE].astype(BF16) for g in range(2)]))
    cb = {(bi, g): _dot_nt(pre[bi]["cm"][g], pre[bi]["bm"][g].astype(BF16))
          for bi in range(nb) for g in range(2)}
    chains = [(bi, p) for bi in range(nb) for p in range(npair)]
    halves = {}
    for bi, p in chains:
        d = pre[bi]
        xdt_bf = d["xdt"][:, p * LANES:(p + 1) * LANES].astype(BF16)
        for hh in range(2):
            ci = ofs + 2 * p + hh
            dec = jnp.exp(jnp.where(tri, d["cum"][:, ci:ci + 1] - d["cum_t"][ci:ci + 1, :], -jnp.inf))
            halves[bi, p, hh] = _dot((cb[bi, p // 2] * dec).astype(BF16), xdt_bf)
    states = [st_ref[bi * npair + p] for bi, p in chains]
    y_off = [_dot(pre[bi]["cm"][p // 2], s.astype(BF16)) for (bi, p), s in zip(chains, states)]
    loc = []
    for bi, p in chains:
        d = pre[bi]
        sl = slice(p * LANES, (p + 1) * LANES)
        loc.append(_dot(d["bm"][p // 2].T.astype(BF16),
                        (d["xdt"][:, sl] * jnp.exp(d["tot_e"][:, sl] - d["cum_e"][:, sl])).astype(BF16)))
    ys = {}
    for n, (bi, p) in enumerate(chains):
        d = pre[bi]
        sl = slice(p * LANES, (p + 1) * LANES)
        st_ref[bi * npair + p] = states[n] * jnp.exp(d["tot_e"][:, sl]) + loc[n]
        ys[bi, p] = (jnp.where(first, halves[bi, p, 0], halves[bi, p, 1]) + y_off[n] * jnp.exp(d["cum_e"][:, sl]))
    for bi in range(nb):
        y = jnp.concatenate([ys[bi, p] for p in range(npair)], axis=1)
        if not reverse:
            o_ref[bi] = y
            continue
        y = y + yf_ref[bi] + pre[bi]["xs"] * dsk_ref[...]
        yg = y * _silu(z_ref[bi])
        nw = nw_ref[...]
        half = BRANCH // 2
        o_ref[bi] = jnp.concatenate(
            [_rms(yg[:, g * half:(g + 1) * half], nw[:, g * half:(g + 1) * half]) for g in range(2)], axis=1)


def _ssd(layer, reverse, xbc, proj, dt_bias, a_log, sel, extra, bsz, seq):
    t = min(SSD_CHUNK, seq)
    nc = seq // t
    rmap = (lambda c: nc - 1 - c) if reverse else (lambda c: c)
    as3d = lambda a: a.reshape(bsz, seq, a.shape[-1])
    tile = lambda width, cb: pl.BlockSpec((bsz, t, width), lambda c: (0, rmap(c), cb))
    in_specs = [tile(BRANCH, 0), tile(BRANCH, 1), tile(LANES, C_DT // LANES),
                _layer_spec((1, LANES), layer),
                _layer_spec((1, LANES), layer),
                pl.BlockSpec((LANES, BRANCH), lambda c: (0, 0))]
    args = [as3d(xbc), as3d(xbc), as3d(proj), dt_bias, a_log, sel]
    if reverse:
        y_f, d_skip, norm_w = extra
        in_specs += [tile(BRANCH, 0), tile(BRANCH, C_Z // BRANCH),
                     _layer_spec((1, BRANCH), layer),
                     _layer_spec((1, BRANCH), layer)]
        args += [as3d(y_f), as3d(proj), d_skip, norm_w]
    out = pl.pallas_call(
        functools.partial(_ssd_body, reverse),
        grid=(nc,),
        in_specs=in_specs,
        out_specs=tile(BRANCH, 0),
        out_shape=jax.ShapeDtypeStruct((bsz, seq, BRANCH), F32),
        scratch_shapes=[pltpu.VMEM((bsz * (BRANCH // LANES), SSM_STATE, LANES), F32)],
        compiler_params=_cparams(("arbitrary",)),
        name="ssd_bwd" if reverse else "ssd_fwd",
    )(*args)
    return out.reshape(bsz * seq, BRANCH)


def _rwkv_operands(d, u0, u_nbr, wa0, wa_nbr, mu_ref, muw_ref, w0_ref, w2_ref, a0_ref, a2_ref, kk_ref, ka_ref, rk_ref,
                   blk):
    def head_sum(x):
        return jnp.concatenate(
            [_dot_sel_r2(x[:, j * LANES:(j + 1) * LANES], blk) for j in range(BRANCH // LANES)], axis=1)

    u = u0 + (u_nbr - u0) * mu_ref[d:d + 1, :]
    wa = wa0 + (wa_nbr - wa0) * muw_ref[d:d + 1, :]
    r, k, v = u[:, 0:BRANCH], u[:, BRANCH:2 * BRANCH], u[:, 2 * BRANCH:3 * BRANCH]
    wlin = w0_ref[d:d + 1, :] + _dot_hp(jnp.tanh(wa), w2_ref[d])
    w = -_softplus(-wlin) - 0.5
    lw = -jnp.exp(w)
    eta = jax.nn.sigmoid(a0_ref[...] + _dot_hp(wa, a2_ref[...]))
    kk = k * kk_ref[...]
    kk = kk / jnp.maximum(jnp.sqrt(head_sum(kk * kk)), 1e-12)
    k2 = k * (1.0 + (eta - 1.0) * ka_ref[...])
    bonus = head_sum(r * k2 * rk_ref[...]) * v
    return r, k2, v, lw, -kk, kk * eta, bonus


def _wkv_body(reverse, *refs):
    (u_ref, uh_ref, wa_ref, wah_ref, mu_ref, muw_ref, w0_ref, w2_ref, a0_ref, a2_ref, kk_ref, ka_ref, rk_ref,
     blk_ref) = refs[:14]
    if reverse:
        yf_ref, bonusf_ref, g_ref, lnw_ref, lnb_ref, o_ref, s_ref, scr, wscr = refs[14:]
    else:
        o_ref, bonus_ref, s_ref, scr, wscr = refs[14:]
    ci = pl.program_id(0)

    @pl.when(ci == 0)
    def _():
        s_ref[...] = jnp.zeros_like(s_ref)

    nb, rows = u_ref.shape[0], u_ref.shape[1]
    direction = 1 if reverse else 0
    halo_at = SUBLANES + rows if reverse else 0
    nbr_at = SUBLANES + 1 if reverse else SUBLANES - 1
    blk = blk_ref[...]
    ops = []
    for bi in range(nb):
        scr[bi, halo_at:halo_at + SUBLANES, :] = jnp.where(ci > 0, uh_ref[bi], 0.0)
        wscr[bi, halo_at:halo_at + SUBLANES, :] = jnp.where(ci > 0, wah_ref[bi], 0.0)
        scr[bi, SUBLANES:SUBLANES + rows, :] = u_ref[bi]
        wscr[bi, SUBLANES:SUBLANES + rows, :] = wa_ref[bi]
        ops.append(_rwkv_operands(direction, u_ref[bi], scr[bi, pl.ds(nbr_at, rows), :], wa_ref[bi],
                                  wscr[bi, pl.ds(nbr_at, rows), :], mu_ref, muw_ref, w0_ref, w2_ref, a0_ref, a2_ref,
                                  kk_ref, ka_ref, rk_ref, blk))
    c = min(WKV_CHUNK, rows)
    nsub = rows // c
    c2 = 2 * c
    npair = BRANCH // LANES
    row = lax.broadcasted_iota(jnp.int32, (c, c), 0)
    col = lax.broadcasted_iota(jnp.int32, (c, c), 1)
    tri_bf = jnp.where((col >= row) if reverse else (col <= row), 1.0, 0.0).astype(BF16)
    lane = lax.broadcasted_iota(jnp.int32, (c, LANES), 1)
    first = lane < HEAD64
    row2 = lax.broadcasted_iota(jnp.int32, (c2, c2), 0)
    col2 = lax.broadcasted_iota(jnp.int32, (c2, c2), 1)
    same = jnp.where(row2 >= c, 1, 0) == jnp.where(col2 >= c, 1, 0)
    if reverse:
        strict, incl = same & (col2 > row2), same & (col2 >= row2)
    else:
        strict, incl = same & (col2 < row2), same & (col2 <= row2)
    diag = row2 == col2
    eye2 = jnp.where(diag, 1.0, 0.0)
    zero_blk = jnp.zeros((c2, LANES), F32)
    squarings = max(1, int(math.ceil(math.log2(c))) - 1)

    def stack(x):
        return jnp.concatenate([jnp.where(first, x, 0.0), jnp.where(first, 0.0, x)], axis=0)

    subs = list(range(nsub))[::-1] if reverse else list(range(nsub))
    chains = [(bi, sub, p) for sub in subs for bi in range(nb) for p in range(npair)]
    pre = {}
    for bi, sub in [(bi, sub) for sub in subs for bi in range(nb)]:
        rs = slice(sub * c, (sub + 1) * c)
        r, k, v, lw, a, b = (x[rs, :] for x in ops[bi][:6])
        cum = _dot_sel_l(tri_bf, lw)
        cum_prev = cum - lw
        tot = cum[0:1, :] if reverse else cum[c - 1:c, :]
        mid = cum[c // 2:c // 2 + 1, :]
        e_in = jnp.exp(mid - cum)
        e_end = jnp.exp(tot - cum)
        pre[bi, sub] = dict(am=a * jnp.exp(cum_prev - mid), bm=b * e_in, km=k * e_in, rm=r * jnp.exp(cum - mid),
                            a0=a * jnp.exp(cum_prev), r0=r * jnp.exp(cum), bend=b * e_end, kend=k * e_end,
                            gdiag=jnp.exp(tot), v=v)

    def part(name, ch):
        bi, sub, p = ch
        return pre[bi, sub][name][:, p * LANES:(p + 1) * LANES]

    v_s = [stack(part("v", ch)) for ch in chains]
    prod = [_dot_nt(jnp.concatenate([stack(part("am", ch)), stack(part("rm", ch))], axis=0).astype(BF16),
                    jnp.concatenate([stack(part("bm", ch)), stack(part("km", ch))], axis=0).astype(BF16))
            for ch in chains]
    l_ab = [jnp.where(strict, x[0:c2, 0:c2], 0.0) for x in prod]
    l_ak = [jnp.where(strict, x[0:c2, c2:2 * c2], 0.0) for x in prod]
    a_rb = [jnp.where(incl, x[c2:2 * c2, 0:c2], 0.0) for x in prod]
    a_rk = [jnp.where(incl, x[c2:2 * c2, c2:2 * c2], 0.0) for x in prod]
    tinv = [eye2 + x for x in l_ab]
    pw = [x.astype(BF16) for x in l_ab]
    pw = [_dot(x, x).astype(BF16) for x in pw]
    lakv = [_dot1(x, y) for x, y in zip(l_ak, v_s)]
    for i in range(squarings):
        if i == squarings - 1:
            tinv = [t + _dot(t.astype(BF16), x) for t, x in zip(tinv, pw)]
        else:
            xs = [_dot(jnp.concatenate([t.astype(BF16), x], axis=0), x) for t, x in zip(tinv, pw)]
            tinv = [t + x[0:c2, :] for t, x in zip(tinv, xs)]
            pw = [x[c2:2 * c2, :].astype(BF16) for x in xs]
    au = [_dot(t.astype(BF16), jnp.concatenate([stack(part("a0", ch)), x], axis=1).astype(BF16))
          for t, x, ch in zip(tinv, lakv, chains)]
    big = []
    for n, ch in enumerate(chains):
        ends_t = jnp.concatenate([stack(part("bend", ch)), stack(part("kend", ch))], axis=0).T
        lhs3 = jnp.concatenate([jnp.concatenate([a_rb[n], a_rk[n]], axis=1), ends_t], axis=0).astype(BF16)
        rhs3 = jnp.concatenate([au[n], jnp.concatenate([zero_blk, v_s[n]], axis=1)], axis=0).astype(BF16)
        big.append(_dot(lhs3, rhs3))
    lhs4 = []
    for n, ch in enumerate(chains):
        r_hat = stack(part("r0", ch)) + big[n][0:c2, 0:LANES]
        gmat = jnp.where(diag, part("gdiag", ch), 0.0) + big[n][c2:c2 + LANES, 0:LANES]
        g_hi, g_lo = _split2(gmat)
        lhs4.append((jnp.concatenate([r_hat.astype(BF16), g_hi], axis=0), g_lo))
    ys = {}
    per_sub = nb * npair
    for si, sub in enumerate(subs):
        idx = range(si * per_sub, (si + 1) * per_sub)
        st = [_split2(s_ref[n - si * per_sub]) for n in idx]
        res = [_dot(lhs4[n][0], s_hi) for n, (s_hi, s_lo) in zip(idx, st)]
        fine = [_dot(lhs4[n][0][c2:c2 + LANES, :], s_lo) + _dot(lhs4[n][1], s_hi) for n, (s_hi, s_lo) in zip(idx, st)]
        for n, x, f in zip(idx, res, fine):
            s_ref[n - si * per_sub] = x[c2:c2 + LANES, :] + f + big[n][c2:c2 + LANES, LANES:2 * LANES]
            y_s = x[0:c2, :] + big[n][0:c2, LANES:2 * LANES]
            ys[chains[n]] = y_s[0:c, :] + y_s[c:c2, :]
    if not reverse:
        for bi in range(nb):
            bonus_ref[bi] = ops[bi][6]
    for bi, sub in [(bi, sub) for sub in subs for bi in range(nb)]:
        rs = slice(sub * c, (sub + 1) * c)
        y = jnp.concatenate([ys[bi, sub, p] for p in range(npair)], axis=1)
        if not reverse:
            o_ref[bi, rs, :] = y
            continue
        wkv = y + yf_ref[bi, rs, :]
        inv_n = 1.0 / HEAD64

        def head_mean(x):
            return jnp.concatenate(
                [_dot_sel_r2(x[:, j * LANES:(j + 1) * LANES], blk) for j in range(npair)], axis=1) * inv_n

        cen = wkv - head_mean(wkv)
        gn = cen * lax.rsqrt(head_mean(cen * cen) + RWKV_GN_EPS) * lnw_ref[...] + lnb_ref[...]
        bonus = bonusf_ref[bi, rs, :] + ops[bi][6][rs, :]
        o_ref[bi, rs, :] = (gn + bonus) * _silu(g_ref[bi, rs, :])


def _wkv(layer, reverse, proj, prm, extra, bsz, seq):
    c = min(WKV_CHUNK * WKV_SUB, seq)
    nc = seq // c
    npair = BRANCH // LANES
    wide = 3 * BRANCH
    per = c // SUBLANES
    rmap = (lambda ci: nc - 1 - ci) if reverse else (lambda ci: ci)
    if reverse:
        hmap = lambda ci: jnp.minimum((rmap(ci) + 1) * per, seq // SUBLANES - 1)
    else:
        hmap = lambda ci: jnp.maximum(rmap(ci) * per - 1, 0)
    as3d = lambda t: t.reshape(bsz, seq, t.shape[-1])
    proj3 = as3d(proj)
    tile = pl.BlockSpec((bsz, c, BRANCH), lambda ci: (0, rmap(ci), 0))
    full = lambda shape: _layer_spec(shape, layer)
    in_specs = [pl.BlockSpec((bsz, c, wide), lambda ci: (0, rmap(ci), C_RKV // wide)),
                pl.BlockSpec((bsz, SUBLANES, wide), lambda ci: (0, hmap(ci), C_RKV // wide)),
                pl.BlockSpec((bsz, c, LANES), lambda ci: (0, rmap(ci), C_WA // LANES)),
                pl.BlockSpec((bsz, SUBLANES, LANES), lambda ci: (0, hmap(ci), C_WA // LANES)),
                full((2, wide)), full((2, LANES)), full((2, BRANCH)), full((2, LANES, BRANCH)),
                full((1, BRANCH)), full((LANES, BRANCH)), full((1, BRANCH)), full((1, BRANCH)), full((1, BRANCH)),
                pl.BlockSpec((LANES, LANES), lambda ci: (0, 0))]
    args = [proj3, proj3, proj3, proj3, prm["mu_rkv"], prm["mu_wa"], prm["w0"], prm["w2"], prm["a0"], prm["a2"],
            prm["k_k"], prm["k_a"], prm["r_k"], prm["blk"]]
    out_sds = jax.ShapeDtypeStruct((bsz, seq, BRANCH), F32)
    if reverse:
        y_f, bonus_f, ln_w, ln_b = extra
        in_specs += [tile, tile,
                     pl.BlockSpec((bsz, c, BRANCH), lambda ci: (0, rmap(ci), C_RG // BRANCH)),
                     full((1, BRANCH)), full((1, BRANCH))]
        args += [y_f, bonus_f, proj3, ln_w, ln_b]
        out_specs, out_shape = tile, out_sds
    else:
        out_specs, out_shape = [tile, tile], [out_sds, out_sds]
    return pl.pallas_call(
        functools.partial(_wkv_body, reverse),
        grid=(nc,),
        in_specs=in_specs,
        out_specs=out_specs,
        out_shape=out_shape,
        scratch_shapes=[pltpu.VMEM((bsz * npair, LANES, LANES), F32),
                        pltpu.VMEM((bsz, c + 2 * SUBLANES, wide), F32),
                        pltpu.VMEM((bsz, c + 2 * SUBLANES, LANES), F32)],
        compiler_params=_cparams(("arbitrary",)),
        name="wkv_bwd" if reverse else "wkv_fwd",
    )(*args)


def _rope(x, cos, sin_signed):
    width = x.shape[1]
    reps = width // LANES
    lane = lax.broadcasted_iota(jnp.int32, x.shape, 1)
    first_half = (lane & (HEAD64 - 1)) < (HEAD64 // 2)
    partner = jnp.where(first_half, pltpu.roll(x, width - HEAD64 // 2, 1), pltpu.roll(x, HEAD64 // 2, 1))
    cos_w = jnp.concatenate([cos] * reps, axis=1) if reps > 1 else cos
    sin_w = jnp.concatenate([sin_signed] * reps, axis=1) if reps > 1 else sin_signed
    return x * cos_w + partner * sin_w


def _attn_prep_body(dq_ref, dk_ref, dv_ref, gq_ref, gk_ref, gv_ref, cd_ref, sd_ref, cg_ref, sg_ref, qw_ref, kw_ref,
                    q1_ref, q2_ref, dko_ref, dvo_ref, gqo_ref, gko_ref, gvo_ref):
    cd, sd, cg, sg = cd_ref[...], sd_ref[...], cg_ref[...], sg_ref[...]
    q = _rope(dq_ref[...], cd, sd) * (HEAD64 ** -0.5 * LOG2E)
    lane = lax.broadcasted_iota(jnp.int32, q.shape, 1)
    comp0 = (lane & (LANES - 1)) < HEAD64
    q1_ref[...] = jnp.where(comp0, q, 0.0).astype(BF16)
    q2_ref[...] = jnp.where(comp0, 0.0, q).astype(BF16)
    dko_ref[...] = _rope(dk_ref[...], cd, sd).astype(BF16)

    def store_vt(v_ref, o_ref):
        vt = v_ref[...].T
        tl = vt.shape[1]
        for h in range(vt.shape[0] // LANES):
            o_ref[0, h, 0:LANES, :] = vt[h * LANES:(h + 1) * LANES, :].astype(BF16)
            o_ref[0, h, LANES:VT_ROWS, :] = jnp.ones((VT_ROWS - LANES, tl), BF16)

    store_vt(dv_ref, dvo_ref)

    def norm_heads(x, w):
        return jnp.concatenate(
            [_rms(x[:, j * LANES:(j + 1) * LANES], w) for j in range(x.shape[1] // LANES)], axis=1)

    gq = _rope(norm_heads(gq_ref[...], qw_ref[...]), cg, sg) * (LANES ** -0.5 * LOG2E)
    gqo_ref[...] = gq.astype(BF16)
    gko_ref[...] = _rope(norm_heads(gk_ref[...], kw_ref[...]), cg, sg).astype(BF16)
    store_vt(gv_ref, gvo_ref)


def _attn_prep(layer, proj, tabs, qw, kw, bsz, seq):
    tl = min(256, seq)
    nt = seq // tl
    m = bsz * seq
    colspec = lambda width, off: pl.BlockSpec((tl, width), lambda b, i: (b * nt + i, off // width))
    tab = pl.BlockSpec((tl, LANES), lambda b, i: (i, 0))
    wspec = _layer_spec((1, LANES), layer)
    o512 = pl.BlockSpec((tl, BRANCH), lambda b, i: (b * nt + i, 0))
    o256 = pl.BlockSpec((tl, 256), lambda b, i: (b * nt + i, 0))
    s512 = jax.ShapeDtypeStruct((m, BRANCH), BF16)
    s256 = jax.ShapeDtypeStruct((m, 256), BF16)
    vt_spec = lambda heads: pl.BlockSpec((1, heads, VT_ROWS, tl), lambda b, i: (b, 0, 0, i))
    vt_shape = lambda heads: jax.ShapeDtypeStruct((bsz, heads, VT_ROWS, seq), BF16)
    return pl.pallas_call(
        _attn_prep_body,
        grid=(bsz, nt),
        in_specs=[colspec(BRANCH, C_DQ), colspec(BRANCH, C_DK), colspec(BRANCH, C_DV), colspec(BRANCH, C_GQ),
                  colspec(256, C_GK), colspec(256, C_GV), tab, tab, tab, tab, wspec, wspec],
        out_specs=[o512, o512, o512, vt_spec(4), o512, o256, vt_spec(2)],
        out_shape=[s512, s512, s512, vt_shape(4), s512, s256, vt_shape(2)],
        compiler_params=_cparams(("parallel", "parallel")),
        name="attn_prep",
    )(proj, proj, proj, proj, proj, proj, *tabs, qw, kw)


def _flash_body(diff, scale_out, ck, *refs):
    if diff:
        qa_ref, qb_ref, k_ref, vt_ref, lam_ref, nw_ref, o_ref, s_scr, p_scr, acc_scr = refs
    else:
        qa_ref, qb_ref, k_ref, vt_ref, o_ref, s_scr, p_scr, acc_scr = refs
    tq = qa_ref.shape[0]
    q = jnp.concatenate([qa_ref[...], qb_ref[...]], axis=0)
    nchunk = k_ref.shape[0] // ck

    def scores(j, slot):
        off = pl.multiple_of(j * ck, ck)
        s = _dot_nt(k_ref[pl.ds(off, ck), :], q)
        s_scr[slot] = s
        return jnp.max(s, axis=0, keepdims=True)

    def pv(j, slot, alpha):
        off = pl.multiple_of(j * ck, ck)
        acc_scr[...] = alpha * acc_scr[...] + _dot(vt_ref[0, 0, :, pl.ds(off, ck)], p_scr[slot])

    def step(j, slot, m_prev, cmax, alpha_prev):
        cmax_next = scores(jnp.minimum(j + 1, nchunk - 1), 1 - slot)
        pv(jnp.maximum(j - 1, 0), 1 - slot, alpha_prev)
        m_new = jnp.maximum(m_prev, cmax)
        p_scr[slot] = jnp.exp2(s_scr[slot] - m_new).astype(BF16)
        return m_new, cmax_next, jnp.exp2(m_prev - m_new)

    def body(i, carry):
        m, cmax, alpha = carry
        m, cmax, alpha = step(2 * i, 0, m, cmax, alpha)
        return step(2 * i + 1, 1, m, cmax, alpha)

    acc_scr[...] = jnp.zeros_like(acc_scr)
    p_scr[1] = jnp.zeros(p_scr.shape[1:], BF16)
    init = (jnp.full((1, 2 * tq), -jnp.inf, F32), scores(0, 0), jnp.ones((1, 2 * tq), F32))
    _, _, alpha_last = lax.fori_loop(0, nchunk // 2, body, init)
    pv(nchunk - 1, 1, alpha_last)
    acc = acc_scr[...]
    o = (acc[0:LANES, :] / acc[LANES:LANES + 1, :]).T
    oa, ob = o[0:tq, :], o[tq:2 * tq, :]
    if diff:
        lp = lam_ref[...]
        lam = (jnp.exp(jnp.sum(lp[0:1, :] * lp[1:2, :], axis=1, keepdims=True))
               - jnp.exp(jnp.sum(lp[2:3, :] * lp[3:4, :], axis=1, keepdims=True)) + (1.0 - scale_out))
        o_ref[...] = _rms(oa - lam * ob, nw_ref[...]) * scale_out
    else:
        o_ref[...] = jnp.concatenate([oa, ob], axis=1)


def _flash(layer, diff, qa, qb, k, vt, extra, lambda_init, bsz, seq):
    tq = min(FLASH_TQ, seq)
    ck = min(FLASH_CK, seq // 2)
    assert seq % (2 * ck) == 0 and seq % tq == 0
    nq = seq // tq
    groups = k.shape[1] // LANES
    qa_spec = pl.BlockSpec((tq, LANES), lambda b, g, i: (b * nq + i, g if diff else 2 * g))
    qb_spec = pl.BlockSpec((tq, LANES), lambda b, g, i: (b * nq + i, g if diff else 2 * g + 1))
    k_spec = pl.BlockSpec((seq, LANES), lambda b, g, i: (b, g))
    vt_spec = pl.BlockSpec((1, 1, VT_ROWS, seq), lambda b, g, i: (b, g, 0, 0))
    in_specs = [qa_spec, qb_spec, k_spec, vt_spec]
    args = [qa, qb, k, vt]
    if diff:
        in_specs += [_layer_spec((4, HEAD64), layer), _layer_spec((1, LANES), layer)]
        args += list(extra)
        out_spec = pl.BlockSpec((tq, LANES), lambda b, g, i: (b * nq + i, g))
    else:
        out_spec = pl.BlockSpec((tq, 2 * LANES), lambda b, g, i: (b * nq + i, g))
    return pl.pallas_call(
        functools.partial(_flash_body, diff, 1.0 - lambda_init, ck),
        grid=(bsz, groups, nq),
        in_specs=in_specs,
        out_specs=out_spec,
        out_shape=jax.ShapeDtypeStruct((bsz * seq, BRANCH), F32),
        scratch_shapes=[pltpu.VMEM((2, ck, 2 * tq), F32), pltpu.VMEM((2, ck, 2 * tq), BF16),
                        pltpu.VMEM((VT_ROWS, 2 * tq), F32)],
        compiler_params=_cparams(("parallel", "parallel", "parallel")),
        name="diff_attn" if diff else "gqa_attn",
    )(*args)


def _outproj_body(x_ref, ya_ref, yb_ref, oc_ref, dg_ref, od_ref, gg_ref, w_ref, pw_ref, o_ref):
    yc = oc_ref[...] * _silu(dg_ref[...])
    yd = od_ref[...] * _silu(gg_ref[...])
    mix = (_dot(ya_ref[...].astype(BF16), w_ref[0]) + _dot(yb_ref[...].astype(BF16), w_ref[1])
           + _dot(yc.astype(BF16), w_ref[2]) + _dot(yd.astype(BF16), w_ref[3]))
    o_ref[...] = x_ref[...] + _rms(mix, pw_ref[...])


def _out_proj(layer, x2, y_a, y_b, o_c, o_d, proj, w4, pw):
    m = x2.shape[0]
    tm = 256
    rows = lambda width, cb: pl.BlockSpec((tm, width), lambda i: (i, cb))
    return pl.pallas_call(
        _outproj_body,
        grid=(m // tm,),
        in_specs=[rows(D_MODEL, 0), rows(BRANCH, 0), rows(BRANCH, 0), rows(BRANCH, 0), rows(BRANCH, C_DG // BRANCH),
                  rows(BRANCH, 0), rows(BRANCH, C_GG // BRANCH),
                  _layer_spec((4, BRANCH, D_MODEL), layer), _layer_spec((1, D_MODEL), layer)],
        out_specs=rows(D_MODEL, 0),
        out_shape=jax.ShapeDtypeStruct((m, D_MODEL), F32),
        compiler_params=_cparams(("parallel",)),
        name="out_proj",
    )(x2, y_a, y_b, o_c, proj, o_d, proj, w4, pw)


_W_IN_PIECES = ((512, 1024), (0, 512), (1552, 1536), (3216, 512), (3728, 2048), (5776, 1536), (3088, 128), (1536, 16))


def _reorder_body(w_ref, o_ref):
    dst = 0
    for src, width in _W_IN_PIECES:
        o_ref[dst:dst + width, :] = w_ref[src:src + width, :].astype(BF16)
        dst += width
    o_ref[dst:N_PROJ, :] = jnp.zeros((N_PROJ - dst, o_ref.shape[1]), BF16)


def _reorder_w_in(w):
    w_t = jnp.swapaxes(w, 1, 2)
    depth, cols, rows = w_t.shape
    tc = 256
    return pl.pallas_call(
        _reorder_body,
        grid=(depth, rows // tc),
        in_specs=[pl.BlockSpec((None, cols, tc), lambda l, i: (l, 0, i))],
        out_specs=pl.BlockSpec((None, N_PROJ, tc), lambda l, i: (l, 0, i)),
        out_shape=jax.ShapeDtypeStruct((depth, N_PROJ, rows), BF16),
        compiler_params=_cparams(("parallel", "parallel")),
        name="w_in_layout",
    )(w_t)


def _rope_tables(seq):
    half = HEAD64 // 2
    inv = (np.float32(ROPE_THETA) ** (-np.arange(half, dtype=np.float32) / np.float32(half))).astype(np.float32)
    t = np.arange(seq, dtype=np.int32)

    def tab(pos):
        ang = (pos.astype(np.float32)[:, None] * inv[None, :]).astype(np.float64)
        cos, sin = np.cos(ang).astype(np.float32), np.sin(ang).astype(np.float32)
        return np.concatenate([cos, cos], axis=1), np.concatenate([-sin, sin], axis=1)

    cp, sp = tab(t)
    cr, sr = tab(t // GRID_W)
    cc, sc = tab(t % GRID_W)
    return tuple(jnp.asarray(a) for a in (np.concatenate([cp, cp], axis=1), np.concatenate([sp, sp], axis=1),
                                          np.concatenate([cr, cc], axis=1), np.concatenate([sr, sc], axis=1)))


def _pad_lanes(v, width=LANES):
    v = v.reshape(v.shape[0], 1, -1)
    return jnp.pad(v, ((0, 0), (0, 0), (0, width - v.shape[2])))


def kernel(x, pre_norm_w, post_norm_w, w_in, w_out, conv_w, conv_b, ssm_a_log, ssm_dt_bias, ssm_d, ssm_norm_w,
           rwkv_mu, rwkv_w0, rwkv_w2, rwkv_a0, rwkv_a2, rwkv_k_k, rwkv_k_a, rwkv_r_k, rwkv_ln_w, rwkv_ln_b,
           diff_lambda, diff_norm_w, gqa_q_norm_w, gqa_k_norm_w):
    bsz, seq, _ = x.shape
    depth = w_in.shape[0]
    m = bsz * seq
    x2 = x.reshape(m, D_MODEL)
    tabs = _rope_tables(seq)
    lane_head = jnp.arange(LANES) // HEAD64
    blk = (lane_head[:, None] == lane_head[None, :]).astype(BF16)
    head_of_lane = jnp.arange(BRANCH) // HEAD64
    sels = [(jnp.arange(LANES)[:, None] == (d * SSM_HEADS + head_of_lane)[None, :]).astype(BF16) for d in range(2)]
    row = lambda v: v.reshape(depth, 1, -1)
    w_in_r = _reorder_w_in(w_in)
    w_out4 = w_out.reshape(depth, 4, BRANCH, D_MODEL).astype(BF16)
    pre_w, post_w = row(pre_norm_w), row(post_norm_w)
    conv_w8 = jnp.pad(conv_w, ((0, 0), (0, SUBLANES - D_CONV), (0, 0)))
    conv_b1 = row(conv_b)
    dt_bias, a_log = _pad_lanes(ssm_dt_bias), _pad_lanes(ssm_a_log)
    d_skip = row(jnp.repeat(ssm_d, HEAD64, axis=-1))
    ssm_nw = row(ssm_norm_w)
    prm = {
        "mu_rkv": rwkv_mu[:, :, :3 * BRANCH], "mu_wa": rwkv_mu[:, :, 3 * BRANCH:], "w0": rwkv_w0,
        "w2": jnp.pad(rwkv_w2, ((0, 0), (0, 0), (0, LANES - RWKV_RANK), (0, 0))),
        "a0": row(rwkv_a0), "a2": jnp.pad(rwkv_a2, ((0, 0), (LANES - RWKV_RANK, 0), (0, 0))),
        "k_k": row(rwkv_k_k), "k_a": row(rwkv_k_a), "r_k": row(rwkv_r_k), "blk": blk,
    }
    ln_w, ln_b = row(rwkv_ln_w), row(rwkv_ln_b)
    diff_nw, gqa_qw, gqa_kw = row(diff_norm_w), row(gqa_q_norm_w), row(gqa_k_norm_w)
    for i in range(depth):
        lambda_init = 0.8 - 0.6 * math.exp(-0.3 * i)
        proj = _in_proj(i, x2, pre_w, w_in_r)
        xbc = _conv(i, proj, conv_w8, conv_b1, bsz, seq)
        y_f = _ssd(i, False, xbc, proj, dt_bias, a_log, sels[0], None, bsz, seq)
        y_a = _ssd(i, True, xbc, proj, dt_bias, a_log, sels[1], (y_f, d_skip, ssm_nw), bsz, seq)
        wkv_f, bonus_f = _wkv(i, False, proj, prm, None, bsz, seq)
        y_b = _wkv(i, True, proj, prm, (wkv_f, bonus_f, ln_w, ln_b), bsz, seq).reshape(m, BRANCH)
        q1, q2, dk, dv, gq, gk, gv = _attn_prep(i, proj, tabs, gqa_qw, gqa_kw, bsz, seq)
        o_c = _flash(i, True, q1, q2, dk, dv, (diff_lambda, diff_nw), lambda_init, bsz, seq)
        o_d = _flash(i, False, gq, gq, gk, gv, None, lambda_init, bsz, seq)
        x2 = _out_proj(i, x2, y_a, y_b, o_c, o_d, proj, w_out4, post_w)
    return x2.reshape(bsz, seq, D_MODEL)
```

```python
import functools
import math

import jax
import jax.numpy as jnp
import numpy as np
from jax import lax
from jax.experimental import pallas as pl
from jax.experimental.pallas import tpu as pltpu

F32 = jnp.float32
BF16 = jnp.bfloat16

D_MODEL = 1024
GRID_W = 64
ROPE_THETA = 10000.0
NORM_EPS = 1e-6
BRANCH = 512
SSM_HEADS = 8
SSM_STATE = 128
D_CONV = 5
RWKV_RANK = 64
RWKV_GN_EPS = 64e-5
HEAD64 = 64
LANES = 128
SUBLANES = 8

C_XBC, C_Z = 0, 1024
C_RKV, C_RG = 1536, 3072
C_DQ, C_DK, C_DV, C_DG = 3584, 4096, 4608, 5120
C_GQ, C_GK, C_GV, C_GG = 5632, 6144, 6400, 6656
C_WA, C_DT = 7168, 7296
N_PROJ = 7424

SSD_CHUNK = 256
WKV_CHUNK = 64
WKV_SUB = 2
FLASH_TQ = 2048
FLASH_CK = 512
FLASH_UNROLL = 1
VT_ROWS = LANES + 16
LOG2E = 1.4426950408889634
VMEM_LIMIT = 48 * 1024 * 1024


def _cparams(sem):
    return pltpu.CompilerParams(dimension_semantics=sem, vmem_limit_bytes=VMEM_LIMIT)


def _layer_spec(shape, layer):
    return pl.BlockSpec((None,) + tuple(shape), lambda *_: (layer,) + (0,) * len(shape))


def _dot(a, b):
    return jnp.dot(a, b, preferred_element_type=F32)


def _dot_nt(a, b):
    return lax.dot_general(a, b, (((1,), (1,)), ((), ())), preferred_element_type=F32)


def _split2(x):
    hi = x.astype(BF16)
    lo = (x - hi.astype(F32)).astype(BF16)
    return hi, lo


def _split3(x):
    hi = x.astype(BF16)
    r1 = x - hi.astype(F32)
    mid = r1.astype(BF16)
    lo = (r1 - mid.astype(F32)).astype(BF16)
    return hi, mid, lo


def _dot_sel_l(sel, x):
    hi, mid, lo = _split3(x)
    return _dot(sel, hi) + _dot(sel, mid) + _dot(sel, lo)


def _dot_sel_r(x, sel):
    hi, mid, lo = _split3(x)
    return _dot(hi, sel) + _dot(mid, sel) + _dot(lo, sel)


def _dot_sel_r2(x, sel):
    hi, lo = _split2(x)
    return _dot(hi, sel) + _dot(lo, sel)


def _dot_hp(a, b):
    ah, al = _split2(a)
    bh, bl = _split2(b)
    return _dot(ah, bh) + _dot(ah, bl) + _dot(al, bh)


def _dot1(a, b):
    return _dot(a.astype(BF16), b.astype(BF16))


def _dot1_nt(a, b):
    return _dot_nt(a.astype(BF16), b.astype(BF16))


def _softplus(x):
    return jnp.maximum(x, 0.0) + jnp.log(1.0 + jnp.exp(-jnp.abs(x)))


def _silu(x):
    return x * jax.nn.sigmoid(x)


def _rms(x, w):
    return x * lax.rsqrt(jnp.mean(x * x, axis=-1, keepdims=True) + NORM_EPS) * w


def _inproj_body(x_ref, nw_ref, w_ref, o_ref):
    h = _rms(x_ref[...], nw_ref[...]).astype(BF16)
    o_ref[...] = _dot_nt(h, w_ref[...])


def _in_proj(layer, x2, nw, w):
    m = x2.shape[0]
    tm, tn = min(512, m), N_PROJ // 2
    return pl.pallas_call(
        _inproj_body,
        grid=(N_PROJ // tn, m // tm),
        in_specs=[pl.BlockSpec((tm, D_MODEL), lambda j, i: (i, 0)),
                  _layer_spec((1, D_MODEL), layer),
                  pl.BlockSpec((None, tn, D_MODEL), lambda j, i: (layer, j, 0))],
        out_specs=pl.BlockSpec((tm, tn), lambda j, i: (i, j)),
        out_shape=jax.ShapeDtypeStruct((m, N_PROJ), F32),
        compiler_params=_cparams(("parallel", "parallel")),
        name="in_proj",
    )(x2, nw, w)


def _halo_specs(tl, width, col_block, nt, nrows):
    per = tl // SUBLANES
    last = nrows // SUBLANES - 1
    cur = pl.BlockSpec((tl, width), lambda b, i: (b * nt + i, col_block))
    prev = pl.BlockSpec((SUBLANES, width), lambda b, i: (jnp.maximum((b * nt + i) * per - 1, 0), col_block))
    nxt = pl.BlockSpec((SUBLANES, width), lambda b, i: (jnp.minimum((b * nt + i + 1) * per, last), col_block))
    return cur, prev, nxt


def _fill_halo(scr, cur_ref, prev_ref, next_ref):
    i = pl.program_id(1)
    tl = cur_ref.shape[0]
    scr[0:SUBLANES, :] = jnp.where(i > 0, prev_ref[...], 0.0)
    scr[SUBLANES:SUBLANES + tl, :] = cur_ref[...]
    scr[SUBLANES + tl:2 * SUBLANES + tl, :] = jnp.where(i < pl.num_programs(1) - 1, next_ref[...], 0.0)


def _conv_body(cur_ref, prev_ref, next_ref, w_ref, b_ref, o_ref, scr):
    _fill_halo(scr, cur_ref, prev_ref, next_ref)
    tl = cur_ref.shape[0]
    acc = b_ref[...] + w_ref[0:1, :] * scr[pl.ds(SUBLANES - 2, tl), :]
    for j in range(1, D_CONV):
        acc = acc + w_ref[j:j + 1, :] * scr[pl.ds(SUBLANES - 2 + j, tl), :]
    o_ref[...] = _silu(acc)


def _conv(layer, proj, conv_w8, conv_b, bsz, seq):
    tl = min(512, seq)
    nt = seq // tl
    width = 1024
    cur, prev, nxt = _halo_specs(tl, width, C_XBC // width, nt, bsz * seq)
    return pl.pallas_call(
        _conv_body,
        grid=(bsz, nt),
        in_specs=[cur, prev, nxt, _layer_spec((SUBLANES, width), layer), _layer_spec((1, width), layer)],
        out_specs=pl.BlockSpec((tl, width), lambda b, i: (b * nt + i, 0)),
        out_shape=jax.ShapeDtypeStruct((bsz * seq, width), F32),
        scratch_shapes=[pltpu.VMEM((tl + 2 * SUBLANES, width), F32)],
        compiler_params=_cparams(("parallel", "parallel")),
        name="ssm_conv",
    )(proj, proj, proj, conv_w8, conv_b)


def _ssd_body(reverse, *refs):
    if reverse:
        (xs_ref, bc_ref, dt_ref, dtb_ref, alog_ref, e_ref, yf_ref, z_ref, dsk_ref, nw_ref, o_ref, st_ref) = refs
    else:
        (xs_ref, bc_ref, dt_ref, dtb_ref, alog_ref, e_ref, o_ref, st_ref) = refs
    c = pl.program_id(0)

    @pl.when(c == 0)
    def _():
        st_ref[...] = jnp.zeros_like(st_ref)

    nb, t = xs_ref.shape[0], xs_ref.shape[1]
    npair = BRANCH // LANES
    row = lax.broadcasted_iota(jnp.int32, (t, t), 0)
    col = lax.broadcasted_iota(jnp.int32, (t, t), 1)
    tri = (col >= row) if reverse else (col <= row)
    tri_bf = jnp.where(tri, 1.0, 0.0).astype(BF16)
    sel = e_ref[...]
    lane = lax.broadcasted_iota(jnp.int32, (t, LANES), 1)
    first = lane < HEAD64
    ofs = SSM_HEADS if reverse else 0
    pre = []
    for bi in range(nb):
        dt_all = _softplus(dt_ref[bi] + dtb_ref[...])
        a_all = dt_all * (-jnp.exp(alog_ref[...]))
        cum = _dot_sel_l(tri_bf, a_all)
        cum_e = _dot_sel_r(cum, sel)
        xs = xs_ref[bi]
        bc = bc_ref[bi]
        pre.append(dict(cum=cum, cum_t=cum.T, cum_e=cum_e, tot_e=cum_e[0:1, :] if reverse else cum_e[t - 1:t, :],
                        xs=xs, xdt=xs * _dot_sel_r(dt_all, sel),
                        bm=[bc[:, g * SSM_STATE:(g + 1) * SSM_STATE] for g in range(2)],
                        cm=[bc[:, 256 + g * SSM_STATE:256 + (g + 1) * SSM_STATE].astype(BF16) for g in range(2)]))
    cb = {(bi, g): _dot_nt(pre[bi]["cm"][g], pre[bi]["bm"][g].astype(BF16))
          for bi in range(nb) for g in range(2)}
    chains = [(bi, p) for bi in range(nb) for p in range(npair)]
    halves = {}
    for bi, p in chains:
        d = pre[bi]
        xdt_bf = d["xdt"][:, p * LANES:(p + 1) * LANES].astype(BF16)
        for hh in range(2):
            ci = ofs + 2 * p + hh
            dec = jnp.exp(jnp.where(tri, d["cum"][:, ci:ci + 1] - d["cum_t"][ci:ci + 1, :], -jnp.inf))
            halves[bi, p, hh] = _dot((cb[bi, p // 2] * dec).astype(BF16), xdt_bf)
    states = [st_ref[bi * npair + p] for bi, p in chains]
    y_off = [_dot(pre[bi]["cm"][p // 2], s.astype(BF16)) for (bi, p), s in zip(chains, states)]
    loc = []
    for bi, p in chains:
        d = pre[bi]
        sl = slice(p * LANES, (p + 1) * LANES)
        loc.append(_dot(d["bm"][p // 2].T.astype(BF16),
                        (d["xdt"][:, sl] * jnp.exp(d["tot_e"][:, sl] - d["cum_e"][:, sl])).astype(BF16)))
    ys = {}
    for n, (bi, p) in enumerate(chains):
        d = pre[bi]
        sl = slice(p * LANES, (p + 1) * LANES)
        st_ref[bi * npair + p] = states[n] * jnp.exp(d["tot_e"][:, sl]) + loc[n]
        ys[bi, p] = (jnp.where(first, halves[bi, p, 0], halves[bi, p, 1]) + y_off[n] * jnp.exp(d["cum_e"][:, sl]))
    for bi in range(nb):
        y = jnp.concatenate([ys[bi, p] for p in range(npair)], axis=1)
        if not reverse:
            o_ref[bi] = y
            continue
        y = y + yf_ref[bi] + pre[bi]["xs"] * dsk_ref[...]
        yg = y * _silu(z_ref[bi])
        nw = nw_ref[...]
        half = BRANCH // 2
        o_ref[bi] = jnp.concatenate(
            [_rms(yg[:, g * half:(g + 1) * half], nw[:, g * half:(g + 1) * half]) for g in range(2)],
            axis=1).astype(o_ref.dtype)


def _ssd(layer, reverse, xbc, proj, dt_bias, a_log, sel, extra, bsz, seq):
    t = min(SSD_CHUNK, seq)
    nc = seq // t
    rmap = (lambda c: nc - 1 - c) if reverse else (lambda c: c)
    as3d = lambda a: a.reshape(bsz, seq, a.shape[-1])
    tile = lambda width, cb: pl.BlockSpec((bsz, t, width), lambda c: (0, rmap(c), cb))
    in_specs = [tile(BRANCH, 0), tile(BRANCH, 1), tile(LANES, C_DT // LANES),
                _layer_spec((1, LANES), layer),
                _layer_spec((1, LANES), layer),
                pl.BlockSpec((LANES, BRANCH), lambda c: (0, 0))]
    args = [as3d(xbc), as3d(xbc), as3d(proj), dt_bias, a_log, sel]
    if reverse:
        y_f, d_skip, norm_w = extra
        in_specs += [tile(BRANCH, 0), tile(BRANCH, C_Z // BRANCH),
                     _layer_spec((1, BRANCH), layer),
                     _layer_spec((1, BRANCH), layer)]
        args += [as3d(y_f), as3d(proj), d_skip, norm_w]
    out = pl.pallas_call(
        functools.partial(_ssd_body, reverse),
        grid=(nc,),
        in_specs=in_specs,
        out_specs=tile(BRANCH, 0),
        out_shape=jax.ShapeDtypeStruct((bsz, seq, BRANCH), BF16 if reverse else F32),
        scratch_shapes=[pltpu.VMEM((bsz * (BRANCH // LANES), SSM_STATE, LANES), F32)],
        compiler_params=_cparams(("arbitrary",)),
        name="ssd_bwd" if reverse else "ssd_fwd",
    )(*args)
    return out.reshape(bsz * seq, BRANCH)


def _rwkv_operands(d, u0, u_nbr, wa0, wa_nbr, mu_ref, muw_ref, w0_ref, w2_ref, a0_ref, a2_ref, kk_ref, ka_ref, rk_ref,
                   blk):
    def head_sum(x):
        return jnp.concatenate(
            [_dot_sel_r2(x[:, j * LANES:(j + 1) * LANES], blk) for j in range(BRANCH // LANES)], axis=1)

    u = u0 + (u_nbr - u0) * mu_ref[d:d + 1, :]
    wa = wa0 + (wa_nbr - wa0) * muw_ref[d:d + 1, :]
    r, k, v = u[:, 0:BRANCH], u[:, BRANCH:2 * BRANCH], u[:, 2 * BRANCH:3 * BRANCH]
    wlin = w0_ref[d:d + 1, :] + _dot_hp(jnp.tanh(wa), w2_ref[d])
    w = -_softplus(-wlin) - 0.5
    lw = -jnp.exp(w)
    eta = jax.nn.sigmoid(a0_ref[...] + _dot_hp(wa, a2_ref[...]))
    kk = k * kk_ref[...]
    kk = kk / jnp.maximum(jnp.sqrt(head_sum(kk * kk)), 1e-12)
    k2 = k * (1.0 + (eta - 1.0) * ka_ref[...])
    bonus = head_sum(r * k2 * rk_ref[...]) * v
    return r, k2, v, lw, -kk, kk * eta, bonus


def _wkv_body(reverse, *refs):
    (u_ref, uh_ref, wa_ref, wah_ref, mu_ref, muw_ref, w0_ref, w2_ref, a0_ref, a2_ref, kk_ref, ka_ref, rk_ref,
     blk_ref) = refs[:14]
    if reverse:
        yf_ref, bonusf_ref, g_ref, lnw_ref, lnb_ref, o_ref, s_ref, scr, wscr = refs[14:]
    else:
        o_ref, bonus_ref, s_ref, scr, wscr = refs[14:]
    ci = pl.program_id(0)

    @pl.when(ci == 0)
    def _():
        s_ref[...] = jnp.zeros_like(s_ref)

    nb, rows = u_ref.shape[0], u_ref.shape[1]
    direction = 1 if reverse else 0
    halo_at = SUBLANES + rows if reverse else 0
    nbr_at = SUBLANES + 1 if reverse else SUBLANES - 1
    blk = blk_ref[...]
    ops = []
    for bi in range(nb):
        scr[bi, halo_at:halo_at + SUBLANES, :] = jnp.where(ci > 0, uh_ref[bi], 0.0)
        wscr[bi, halo_at:halo_at + SUBLANES, :] = jnp.where(ci > 0, wah_ref[bi], 0.0)
        scr[bi, SUBLANES:SUBLANES + rows, :] = u_ref[bi]
        wscr[bi, SUBLANES:SUBLANES + rows, :] = wa_ref[bi]
        ops.append(_rwkv_operands(direction, u_ref[bi], scr[bi, pl.ds(nbr_at, rows), :], wa_ref[bi],
                                  wscr[bi, pl.ds(nbr_at, rows), :], mu_ref, muw_ref, w0_ref, w2_ref, a0_ref, a2_ref,
                                  kk_ref, ka_ref, rk_ref, blk))
    c = min(WKV_CHUNK, rows)
    nsub = rows // c
    c2 = 2 * c
    npair = BRANCH // LANES
    row = lax.broadcasted_iota(jnp.int32, (c, c), 0)
    col = lax.broadcasted_iota(jnp.int32, (c, c), 1)
    tri_bf = jnp.where((col >= row) if reverse else (col <= row), 1.0, 0.0).astype(BF16)
    lane = lax.broadcasted_iota(jnp.int32, (c, LANES), 1)
    first = lane < HEAD64
    row2 = lax.broadcasted_iota(jnp.int32, (c2, c2), 0)
    col2 = lax.broadcasted_iota(jnp.int32, (c2, c2), 1)
    same = jnp.where(row2 >= c, 1, 0) == jnp.where(col2 >= c, 1, 0)
    if reverse:
        strict, incl = same & (col2 > row2), same & (col2 >= row2)
    else:
        strict, incl = same & (col2 < row2), same & (col2 <= row2)
    diag = row2 == col2
    eye2 = jnp.where(diag, 1.0, 0.0)
    zero_blk = jnp.zeros((c2, LANES), F32)
    squarings = max(1, int(math.ceil(math.log2(c))) - 1)

    def stack(x):
        return jnp.concatenate([jnp.where(first, x, 0.0), jnp.where(first, 0.0, x)], axis=0)

    subs = list(range(nsub))[::-1] if reverse else list(range(nsub))
    chains = [(bi, sub, p) for sub in subs for bi in range(nb) for p in range(npair)]
    pre = {}
    for bi, sub in [(bi, sub) for sub in subs for bi in range(nb)]:
        rs = slice(sub * c, (sub + 1) * c)
        r, k, v, lw, a, b = (x[rs, :] for x in ops[bi][:6])
        cum = _dot_sel_l(tri_bf, lw)
        cum_prev = cum - lw
        tot = cum[0:1, :] if reverse else cum[c - 1:c, :]
        mid = cum[c // 2:c // 2 + 1, :]
        e_in = jnp.exp(mid - cum)
        e_end = jnp.exp(tot - cum)
        pre[bi, sub] = dict(am=a * jnp.exp(cum_prev - mid), bm=b * e_in, km=k * e_in, rm=r * jnp.exp(cum - mid),
                            a0=a * jnp.exp(cum_prev), r0=r * jnp.exp(cum), bend=b * e_end, kend=k * e_end,
                            gdiag=jnp.exp(tot), v=v)

    def part(name, ch):
        bi, sub, p = ch
        return pre[bi, sub][name][:, p * LANES:(p + 1) * LANES]

    v_s = [stack(part("v", ch)) for ch in chains]
    prod = [_dot_nt(jnp.concatenate([stack(part("am", ch)), stack(part("rm", ch))], axis=0).astype(BF16),
                    jnp.concatenate([stack(part("bm", ch)), stack(part("km", ch))], axis=0).astype(BF16))
            for ch in chains]
    l_ab = [jnp.where(strict, x[0:c2, 0:c2], 0.0) for x in prod]
    l_ak = [jnp.where(strict, x[0:c2, c2:2 * c2], 0.0) for x in prod]
    a_rb = [jnp.where(incl, x[c2:2 * c2, 0:c2], 0.0) for x in prod]
    a_rk = [jnp.where(incl, x[c2:2 * c2, c2:2 * c2], 0.0) for x in prod]
    tinv = [eye2 + x for x in l_ab]
    pw = [x.astype(BF16) for x in l_ab]
    pw = [_dot(x, x).astype(BF16) for x in pw]
    lakv = [_dot1(x, y) for x, y in zip(l_ak, v_s)]
    for i in range(squarings):
        if i == squarings - 1:
            tinv = [t + _dot(t.astype(BF16), x) for t, x in zip(tinv, pw)]
        else:
            xs = [_dot(jnp.concatenate([t.astype(BF16), x], axis=0), x) for t, x in zip(tinv, pw)]
            tinv = [t + x[0:c2, :] for t, x in zip(tinv, xs)]
            pw = [x[c2:2 * c2, :].astype(BF16) for x in xs]
    au = [_dot(t.astype(BF16), jnp.concatenate([stack(part("a0", ch)), x], axis=1).astype(BF16))
          for t, x, ch in zip(tinv, lakv, chains)]
    big = []
    for n, ch in enumerate(chains):
        ends_t = jnp.concatenate([stack(part("bend", ch)), stack(part("kend", ch))], axis=0).T
        lhs3 = jnp.concatenate([jnp.concatenate([a_rb[n], a_rk[n]], axis=1), ends_t], axis=0).astype(BF16)
        rhs3 = jnp.concatenate([au[n], jnp.concatenate([zero_blk, v_s[n]], axis=1)], axis=0).astype(BF16)
        big.append(_dot(lhs3, rhs3))
    lhs4 = []
    for n, ch in enumerate(chains):
        r_hat = stack(part("r0", ch)) + big[n][0:c2, 0:LANES]
        gmat = jnp.where(diag, part("gdiag", ch), 0.0) + big[n][c2:c2 + LANES, 0:LANES]
        g_hi, g_lo = _split2(gmat)
        lhs4.append((jnp.concatenate([r_hat.astype(BF16), g_hi], axis=0), g_lo))
    ys = {}
    per_sub = nb * npair
    for si, sub in enumerate(subs):
        idx = range(si * per_sub, (si + 1) * per_sub)
        st = [_split2(s_ref[n - si * per_sub]) for n in idx]
        res = [_dot(lhs4[n][0], s_hi) for n, (s_hi, s_lo) in zip(idx, st)]
        fine = [_dot(lhs4[n][0][c2:c2 + LANES, :], s_lo) + _dot(lhs4[n][1], s_hi) for n, (s_hi, s_lo) in zip(idx, st)]
        for n, x, f in zip(idx, res, fine):
            s_ref[n - si * per_sub] = x[c2:c2 + LANES, :] + f + big[n][c2:c2 + LANES, LANES:2 * LANES]
            y_s = x[0:c2, :] + big[n][0:c2, LANES:2 * LANES]
            ys[chains[n]] = y_s[0:c, :] + y_s[c:c2, :]
    if not reverse:
        for bi in range(nb):
            bonus_ref[bi] = ops[bi][6]
    for bi, sub in [(bi, sub) for sub in subs for bi in range(nb)]:
        rs = slice(sub * c, (sub + 1) * c)
        y = jnp.concatenate([ys[bi, sub, p] for p in range(npair)], axis=1)
        if not reverse:
            o_ref[bi, rs, :] = y
            continue
        wkv = y + yf_ref[bi, rs, :]
        inv_n = 1.0 / HEAD64

        def head_mean(x):
            return jnp.concatenate(
                [_dot_sel_r2(x[:, j * LANES:(j + 1) * LANES], blk) for j in range(npair)], axis=1) * inv_n

        cen = wkv - head_mean(wkv)
        gn = cen * lax.rsqrt(head_mean(cen * cen) + RWKV_GN_EPS) * lnw_ref[...] + lnb_ref[...]
        bonus = bonusf_ref[bi, rs, :] + ops[bi][6][rs, :]
        o_ref[bi, rs, :] = ((gn + bonus) * _silu(g_ref[bi, rs, :])).astype(o_ref.dtype)


def _wkv(layer, reverse, proj, prm, extra, bsz, seq):
    c = min(WKV_CHUNK * WKV_SUB, seq)
    nc = seq // c
    npair = BRANCH // LANES
    wide = 3 * BRANCH
    per = c // SUBLANES
    rmap = (lambda ci: nc - 1 - ci) if reverse else (lambda ci: ci)
    if reverse:
        hmap = lambda ci: jnp.minimum((rmap(ci) + 1) * per, seq // SUBLANES - 1)
    else:
        hmap = lambda ci: jnp.maximum(rmap(ci) * per - 1, 0)
    as3d = lambda t: t.reshape(bsz, seq, t.shape[-1])
    proj3 = as3d(proj)
    tile = pl.BlockSpec((bsz, c, BRANCH), lambda ci: (0, rmap(ci), 0))
    full = lambda shape: _layer_spec(shape, layer)
    in_specs = [pl.BlockSpec((bsz, c, wide), lambda ci: (0, rmap(ci), C_RKV // wide)),
                pl.BlockSpec((bsz, SUBLANES, wide), lambda ci: (0, hmap(ci), C_RKV // wide)),
                pl.BlockSpec((bsz, c, LANES), lambda ci: (0, rmap(ci), C_WA // LANES)),
                pl.BlockSpec((bsz, SUBLANES, LANES), lambda ci: (0, hmap(ci), C_WA // LANES)),
                full((2, wide)), full((2, LANES)), full((2, BRANCH)), full((2, LANES, BRANCH)),
                full((1, BRANCH)), full((LANES, BRANCH)), full((1, BRANCH)), full((1, BRANCH)), full((1, BRANCH)),
                pl.BlockSpec((LANES, LANES), lambda ci: (0, 0))]
    args = [proj3, proj3, proj3, proj3, prm["mu_rkv"], prm["mu_wa"], prm["w0"], prm["w2"], prm["a0"], prm["a2"],
            prm["k_k"], prm["k_a"], prm["r_k"], prm["blk"]]
    out_sds = jax.ShapeDtypeStruct((bsz, seq, BRANCH), F32)
    if reverse:
        y_f, bonus_f, ln_w, ln_b = extra
        in_specs += [tile, tile,
                     pl.BlockSpec((bsz, c, BRANCH), lambda ci: (0, rmap(ci), C_RG // BRANCH)),
                     full((1, BRANCH)), full((1, BRANCH))]
        args += [y_f, bonus_f, proj3, ln_w, ln_b]
        out_specs, out_shape = tile, jax.ShapeDtypeStruct((bsz, seq, BRANCH), BF16)
    else:
        out_specs, out_shape = [tile, tile], [out_sds, out_sds]
    return pl.pallas_call(
        functools.partial(_wkv_body, reverse),
        grid=(nc,),
        in_specs=in_specs,
        out_specs=out_specs,
        out_shape=out_shape,
        scratch_shapes=[pltpu.VMEM((bsz * npair, LANES, LANES), F32),
                        pltpu.VMEM((bsz, c + 2 * SUBLANES, wide), F32),
                        pltpu.VMEM((bsz, c + 2 * SUBLANES, LANES), F32)],
        compiler_params=_cparams(("arbitrary",)),
        name="wkv_bwd" if reverse else "wkv_fwd",
    )(*args)


def _rope(x, cos, sin_signed):
    blocks = []
    for j in range(x.shape[1] // LANES):
        xb = x[:, j * LANES:(j + 1) * LANES]
        blocks.append(xb * cos + pltpu.roll(xb, LANES // 2, 1) * sin_signed)
    return jnp.concatenate(blocks, axis=1) if len(blocks) > 1 else blocks[0]


def _attn_prep_body(dq_ref, dk_ref, dv_ref, gq_ref, gk_ref, gv_ref, cd_ref, sd_ref, cg_ref, sg_ref, qw_ref, kw_ref,
                    q1_ref, q2_ref, dko_ref, dvo_ref, gqo_ref, gko_ref, gvo_ref):
    cd, sd, cg, sg = cd_ref[...], sd_ref[...], cg_ref[...], sg_ref[...]
    q = _rope(dq_ref[...], cd, sd) * (HEAD64 ** -0.5 * LOG2E)
    lane = lax.broadcasted_iota(jnp.int32, q.shape, 1)
    comp0 = (lane & (HEAD64 // 2)) == 0
    q1_ref[...] = jnp.where(comp0, q, 0.0).astype(BF16)
    q2_ref[...] = jnp.where(comp0, 0.0, q).astype(BF16)
    dko_ref[...] = _rope(dk_ref[...], cd, sd).astype(BF16)

    def store_vt(v_ref, o_ref):
        vt = v_ref[...].T
        tl = vt.shape[1]
        for h in range(vt.shape[0] // LANES):
            o_ref[0, h, 0:LANES, :] = vt[h * LANES:(h + 1) * LANES, :].astype(BF16)
            o_ref[0, h, LANES:VT_ROWS, :] = jnp.ones((VT_ROWS - LANES, tl), BF16)

    store_vt(dv_ref, dvo_ref)

    def norm_heads(x, w):
        return jnp.concatenate(
            [_rms(x[:, j * LANES:(j + 1) * LANES], w) for j in range(x.shape[1] // LANES)], axis=1)

    gq = _rope(norm_heads(gq_ref[...], qw_ref[...]), cg, sg) * (LANES ** -0.5 * LOG2E)
    gqo_ref[...] = gq.astype(BF16)
    gko_ref[...] = _rope(norm_heads(gk_ref[...], kw_ref[...]), cg, sg).astype(BF16)
    store_vt(gv_ref, gvo_ref)


def _attn_prep(layer, proj, tabs, qw, kw, bsz, seq):
    tl = min(256, seq)
    nt = seq // tl
    m = bsz * seq
    colspec = lambda width, off: pl.BlockSpec((tl, width), lambda b, i: (b * nt + i, off // width))
    tab = pl.BlockSpec((tl, LANES), lambda b, i: (i, 0))
    wspec = _layer_spec((1, LANES), layer)
    o512 = pl.BlockSpec((tl, BRANCH), lambda b, i: (b * nt + i, 0))
    o256 = pl.BlockSpec((tl, 256), lambda b, i: (b * nt + i, 0))
    s512 = jax.ShapeDtypeStruct((m, BRANCH), BF16)
    s256 = jax.ShapeDtypeStruct((m, 256), BF16)
    vt_spec = lambda heads: pl.BlockSpec((1, heads, VT_ROWS, tl), lambda b, i: (b, 0, 0, i))
    vt_shape = lambda heads: jax.ShapeDtypeStruct((bsz, heads, VT_ROWS, seq), BF16)
    return pl.pallas_call(
        _attn_prep_body,
        grid=(bsz, nt),
        in_specs=[colspec(BRANCH, C_DQ), colspec(BRANCH, C_DK), colspec(BRANCH, C_DV), colspec(BRANCH, C_GQ),
                  colspec(256, C_GK), colspec(256, C_GV), tab, tab, tab, tab, wspec, wspec],
        out_specs=[o512, o512, o512, vt_spec(4), o512, o256, vt_spec(2)],
        out_shape=[s512, s512, s512, vt_shape(4), s512, s256, vt_shape(2)],
        compiler_params=_cparams(("parallel", "parallel")),
        name="attn_prep",
    )(proj, proj, proj, proj, proj, proj, *tabs, qw, kw)


def _flash_body(diff, scale_out, ck, *refs):
    if diff:
        qa_ref, qb_ref, k_ref, vt_ref, gate_ref, lam_ref, nw_ref, o_ref, s_scr, p_scr, acc_scr = refs
    else:
        qa_ref, qb_ref, k_ref, vt_ref, gate_ref, o_ref, s_scr, p_scr, acc_scr = refs
    tq = qa_ref.shape[0]
    q = jnp.concatenate([qa_ref[...], qb_ref[...]], axis=0)
    nchunk = k_ref.shape[0] // ck

    def scores(j, slot):
        off = pl.multiple_of(j * ck, ck)
        s = _dot_nt(k_ref[pl.ds(off, ck), :], q)
        s_scr[slot] = s
        return jnp.max(s, axis=0, keepdims=True)

    def pv(j, slot, alpha):
        off = pl.multiple_of(j * ck, ck)
        acc_scr[...] = alpha * acc_scr[...] + _dot(vt_ref[0, 0, :, pl.ds(off, ck)], p_scr[slot])

    def step(j, slot, m_prev, cmax, alpha_prev):
        cmax_next = scores(jnp.minimum(j + 1, nchunk - 1), 1 - slot)
        pv(jnp.maximum(j - 1, 0), 1 - slot, alpha_prev)
        m_new = jnp.maximum(m_prev, cmax)
        p_scr[slot] = jnp.exp2(s_scr[slot] - m_new).astype(BF16)
        return m_new, cmax_next, jnp.exp2(m_prev - m_new)

    def body(i, carry):
        m, cmax, alpha = carry
        m, cmax, alpha = step(2 * i, 0, m, cmax, alpha)
        return step(2 * i + 1, 1, m, cmax, alpha)

    acc_scr[...] = jnp.zeros_like(acc_scr)
    p_scr[1] = jnp.zeros(p_scr.shape[1:], BF16)
    init = (jnp.full((1, 2 * tq), -jnp.inf, F32), scores(0, 0), jnp.ones((1, 2 * tq), F32))
    _, _, alpha_last = lax.fori_loop(0, nchunk // 2, body, init)
    pv(nchunk - 1, 1, alpha_last)
    acc = acc_scr[...]
    o = (acc[0:LANES, :] / acc[LANES:LANES + 1, :]).T
    oa, ob = o[0:tq, :], o[tq:2 * tq, :]
    if diff:
        lp = lam_ref[...]
        lam = (jnp.exp(jnp.sum(lp[0:1, :] * lp[1:2, :], axis=1, keepdims=True))
               - jnp.exp(jnp.sum(lp[2:3, :] * lp[3:4, :], axis=1, keepdims=True)) + (1.0 - scale_out))
        out = _rms(oa - lam * ob, nw_ref[...]) * scale_out
    else:
        out = jnp.concatenate([oa, ob], axis=1)
    o_ref[...] = (out * _silu(gate_ref[...])).astype(o_ref.dtype)


def _flash(layer, diff, qa, qb, k, vt, proj, extra, lambda_init, bsz, seq):
    tq = min(FLASH_TQ, seq)
    ck = min(FLASH_CK, seq // 2)
    assert seq % (2 * ck) == 0 and seq % tq == 0
    nq = seq // tq
    groups = k.shape[1] // LANES
    out_w = LANES if diff else 2 * LANES
    gate_col = (C_DG if diff else C_GG) // out_w
    qa_spec = pl.BlockSpec((tq, LANES), lambda b, g, i: (b * nq + i, g if diff else 2 * g))
    qb_spec = pl.BlockSpec((tq, LANES), lambda b, g, i: (b * nq + i, g if diff else 2 * g + 1))
    k_spec = pl.BlockSpec((seq, LANES), lambda b, g, i: (b, g))
    vt_spec = pl.BlockSpec((1, 1, VT_ROWS, seq), lambda b, g, i: (b, g, 0, 0))
    gate_spec = pl.BlockSpec((tq, out_w), lambda b, g, i: (b * nq + i, gate_col + g))
    in_specs = [qa_spec, qb_spec, k_spec, vt_spec, gate_spec]
    args = [qa, qb, k, vt, proj]
    if diff:
        in_specs += [_layer_spec((4, HEAD64), layer), _layer_spec((1, LANES), layer)]
        args += list(extra)
    out_spec = pl.BlockSpec((tq, out_w), lambda b, g, i: (b * nq + i, g))
    return pl.pallas_call(
        functools.partial(_flash_body, diff, 1.0 - lambda_init, ck),
        grid=(bsz, groups, nq),
        in_specs=in_specs,
        out_specs=out_spec,
        out_shape=jax.ShapeDtypeStruct((bsz * seq, BRANCH), BF16),
        scratch_shapes=[pltpu.VMEM((2, ck, 2 * tq), F32), pltpu.VMEM((2, ck, 2 * tq), BF16),
                        pltpu.VMEM((VT_ROWS, 2 * tq), F32)],
        compiler_params=_cparams(("parallel", "parallel", "parallel")),
        name="diff_attn" if diff else "gqa_attn",
    )(*args)


def _outproj_body(x_ref, ya_ref, yb_ref, yc_ref, yd_ref, w_ref, pw_ref, o_ref):
    mix = (_dot(ya_ref[...], w_ref[0]) + _dot(yb_ref[...], w_ref[1])
           + _dot(yc_ref[...], w_ref[2]) + _dot(yd_ref[...], w_ref[3]))
    o_ref[...] = x_ref[...] + _rms(mix, pw_ref[...])


def _out_proj(layer, x2, y_a, y_b, y_c, y_d, w4, pw):
    m = x2.shape[0]
    tm = min(512, m)
    rows = lambda width: pl.BlockSpec((tm, width), lambda i: (i, 0))
    return pl.pallas_call(
        _outproj_body,
        grid=(m // tm,),
        in_specs=[rows(D_MODEL), rows(BRANCH), rows(BRANCH), rows(BRANCH), rows(BRANCH),
                  _layer_spec((4, BRANCH, D_MODEL), layer), _layer_spec((1, D_MODEL), layer)],
        out_specs=rows(D_MODEL),
        out_shape=jax.ShapeDtypeStruct((m, D_MODEL), F32),
        compiler_params=_cparams(("parallel",)),
        name="out_proj",
    )(x2, y_a, y_b, y_c, y_d, w4, pw)


def _rotary_cols(start, width):
    quarter = HEAD64 // 2
    return tuple((start + blk + q * quarter, quarter) for blk in range(0, width, LANES) for q in (0, 2, 1, 3))


_W_IN_PIECES = (((512, 1024), (0, 512), (1552, 1536), (3216, 512))
                + _rotary_cols(3728, 1024) + ((4752, 1024),)
                + _rotary_cols(5776, 768) + ((6544, 768),)
                + ((3088, 128), (1536, 16)))


def _reorder_body(w_ref, o_ref):
    dst = 0
    for src, width in _W_IN_PIECES:
        o_ref[dst:dst + width, :] = w_ref[src:src + width, :].astype(BF16)
        dst += width
    o_ref[dst:N_PROJ, :] = jnp.zeros((N_PROJ - dst, o_ref.shape[1]), BF16)


def _reorder_w_in(w):
    w_t = jnp.swapaxes(w, 1, 2)
    depth, cols, rows = w_t.shape
    tc = 256
    return pl.pallas_call(
        _reorder_body,
        grid=(depth, rows // tc),
        in_specs=[pl.BlockSpec((None, cols, tc), lambda l, i: (l, 0, i))],
        out_specs=pl.BlockSpec((None, N_PROJ, tc), lambda l, i: (l, 0, i)),
        out_shape=jax.ShapeDtypeStruct((depth, N_PROJ, rows), BF16),
        compiler_params=_cparams(("parallel", "parallel")),
        name="w_in_layout",
    )(w_t)


def _rope_tables(seq):
    half = HEAD64 // 2
    inv = (np.float32(ROPE_THETA) ** (-np.arange(half, dtype=np.float32) / np.float32(half))).astype(np.float32)
    t = np.arange(seq, dtype=np.int32)

    def tab(pos):
        ang = (pos.astype(np.float32)[:, None] * inv[None, :]).astype(np.float64)
        return np.cos(ang).astype(np.float32), np.sin(ang).astype(np.float32)

    def lanes(g1, g2):
        (c1, s1), (c2, s2) = g1, g2
        return np.concatenate([c1, c2, c1, c2], axis=1), np.concatenate([-s1, -s2, s1, s2], axis=1)

    p, r, c = tab(t), tab(t // GRID_W), tab(t % GRID_W)
    return tuple(jnp.asarray(a) for a in lanes(p, p) + lanes(r, c))


def _pad_lanes(v, width=LANES):
    v = v.reshape(v.shape[0], 1, -1)
    return jnp.pad(v, ((0, 0), (0, 0), (0, width - v.shape[2])))


def kernel(x, pre_norm_w, post_norm_w, w_in, w_out, conv_w, conv_b, ssm_a_log, ssm_dt_bias, ssm_d, ssm_norm_w,
           rwkv_mu, rwkv_w0, rwkv_w2, rwkv_a0, rwkv_a2, rwkv_k_k, rwkv_k_a, rwkv_r_k, rwkv_ln_w, rwkv_ln_b,
           diff_lambda, diff_norm_w, gqa_q_norm_w, gqa_k_norm_w):
    bsz, seq, _ = x.shape
    depth = w_in.shape[0]
    m = bsz * seq
    x2 = x.reshape(m, D_MODEL)
    tabs = _rope_tables(seq)
    lane_head = jnp.arange(LANES) // HEAD64
    blk = (lane_head[:, None] == lane_head[None, :]).astype(BF16)
    head_of_lane = jnp.arange(BRANCH) // HEAD64
    sels = [(jnp.arange(LANES)[:, None] == (d * SSM_HEADS + head_of_lane)[None, :]).astype(BF16) for d in range(2)]
    row = lambda v: v.reshape(depth, 1, -1)
    w_in_r = _reorder_w_in(w_in)
    w_out4 = w_out.reshape(depth, 4, BRANCH, D_MODEL).astype(BF16)
    pre_w, post_w = row(pre_norm_w), row(post_norm_w)
    conv_w8 = jnp.pad(conv_w, ((0, 0), (0, SUBLANES - D_CONV), (0, 0)))
    conv_b1 = row(conv_b)
    dt_bias, a_log = _pad_lanes(ssm_dt_bias), _pad_lanes(ssm_a_log)
    d_skip = row(jnp.repeat(ssm_d, HEAD64, axis=-1))
    ssm_nw = row(ssm_norm_w)
    prm = {
        "mu_rkv": rwkv_mu[:, :, :3 * BRANCH], "mu_wa": rwkv_mu[:, :, 3 * BRANCH:], "w0": rwkv_w0,
        "w2": jnp.pad(rwkv_w2, ((0, 0), (0, 0), (0, LANES - RWKV_RANK), (0, 0))),
        "a0": row(rwkv_a0), "a2": jnp.pad(rwkv_a2, ((0, 0), (LANES - RWKV_RANK, 0), (0, 0))),
        "k_k": row(rwkv_k_k), "k_a": row(rwkv_k_a), "r_k": row(rwkv_r_k), "blk": blk,
    }
    ln_w, ln_b = row(rwkv_ln_w), row(rwkv_ln_b)
    rotary_order = lambda v: row(jnp.swapaxes(v.reshape(depth, 2, 2, HEAD64 // 2), 1, 2))
    diff_nw, gqa_qw, gqa_kw = row(diff_norm_w), rotary_order(gqa_q_norm_w), rotary_order(gqa_k_norm_w)
    for i in range(depth):
        lambda_init = 0.8 - 0.6 * math.exp(-0.3 * i)
        proj = _in_proj(i, x2, pre_w, w_in_r)
        xbc = _conv(i, proj, conv_w8, conv_b1, bsz, seq)
        y_f = _ssd(i, False, xbc, proj, dt_bias, a_log, sels[0], None, bsz, seq)
        y_a = _ssd(i, True, xbc, proj, dt_bias, a_log, sels[1], (y_f, d_skip, ssm_nw), bsz, seq)
        wkv_f, bonus_f = _wkv(i, False, proj, prm, None, bsz, seq)
        y_b = _wkv(i, True, proj, prm, (wkv_f, bonus_f, ln_w, ln_b), bsz, seq).reshape(m, BRANCH)
        q1, q2, dk, dv, gq, gk, gv = _attn_prep(i, proj, tabs, gqa_qw, gqa_kw, bsz, seq)
        y_c = _flash(i, True, q1, q2, dk, dv, proj, (diff_lambda, diff_nw), lambda_init, bsz, seq)
        y_d = _flash(i, False, gq, gq, gk, gv, proj, None, lambda_init, bsz, seq)
        x2 = _out_proj(i, x2, y_a, y_b, y_c, y_d, w_out4, post_w)
    return x2.reshape(bsz, seq, D_MODEL)
```

```python
import functools
import math

import jax
import jax.numpy as jnp
import numpy as np
from jax import lax
from jax.experimental import pallas as pl
from jax.experimental.pallas import tpu as pltpu

F32 = jnp.float32
BF16 = jnp.bfloat16

D_MODEL = 1024
GRID_W = 64
ROPE_THETA = 10000.0
NORM_EPS = 1e-6
BRANCH = 512
SSM_HEADS = 8
SSM_STATE = 128
D_CONV = 5
RWKV_RANK = 64
RWKV_GN_EPS = 64e-5
HEAD64 = 64
LANES = 128
SUBLANES = 8

C_XBC, C_Z = 0, 1024
C_RKV, C_RG = 1536, 3072
C_DQ, C_DK, C_DV, C_DG = 3584, 4096, 4608, 5120
C_GQ, C_GK, C_GV, C_GG = 5632, 6144, 6400, 6656
C_WA, C_DT = 7168, 7296
N_PROJ = 7424

SSD_CHUNK = 256
WKV_CHUNK = 64
WKV_SUB = 4
FLASH_TQ = 2048
FLASH_CK = 512
VT_ROWS = LANES + 16
LOG2E = 1.4426950408889634
VMEM_LIMIT = 48 * 1024 * 1024


def _cparams(sem):
    return pltpu.CompilerParams(dimension_semantics=sem, vmem_limit_bytes=VMEM_LIMIT)


def _layer_spec(shape, layer):
    return pl.BlockSpec((None,) + tuple(shape), lambda *_: (layer,) + (0,) * len(shape))


def _dot(a, b):
    return jnp.dot(a, b, preferred_element_type=F32)


def _dot_nt(a, b):
    return lax.dot_general(a, b, (((1,), (1,)), ((), ())), preferred_element_type=F32)


def _split2(x):
    hi = x.astype(BF16)
    lo = (x - hi.astype(F32)).astype(BF16)
    return hi, lo


def _split3(x):
    hi = x.astype(BF16)
    r1 = x - hi.astype(F32)
    mid = r1.astype(BF16)
    lo = (r1 - mid.astype(F32)).astype(BF16)
    return hi, mid, lo


def _dot_sel_l(sel, x):
    hi, mid, lo = _split3(x)
    return _dot(sel, hi) + _dot(sel, mid) + _dot(sel, lo)


def _dot_sel_r(x, sel):
    hi, mid, lo = _split3(x)
    return _dot(hi, sel) + _dot(mid, sel) + _dot(lo, sel)


def _dot_sel_r2(x, sel):
    hi, lo = _split2(x)
    return _dot(hi, sel) + _dot(lo, sel)


def _dot_hp(a, b):
    ah, al = _split2(a)
    bh, bl = _split2(b)
    return _dot(ah, bh) + _dot(ah, bl) + _dot(al, bh)


def _dot1(a, b):
    return _dot(a.astype(BF16), b.astype(BF16))


def _softplus(x):
    return jnp.maximum(x, 0.0) + jnp.log(1.0 + jnp.exp(-jnp.abs(x)))


def _silu(x):
    return x * jax.nn.sigmoid(x)


def _rms(x, w):
    return x * lax.rsqrt(jnp.mean(x * x, axis=-1, keepdims=True) + NORM_EPS) * w


def _inproj_body(x_ref, nw_ref, w_ref, o_ref):
    h = _rms(x_ref[...], nw_ref[...]).astype(BF16)
    o_ref[...] = _dot_nt(h, w_ref[...])


def _in_proj(layer, x2, nw, w):
    m = x2.shape[0]
    tm, tn = min(512, m), N_PROJ // 2
    return pl.pallas_call(
        _inproj_body,
        grid=(N_PROJ // tn, m // tm),
        in_specs=[pl.BlockSpec((tm, D_MODEL), lambda j, i: (i, 0)),
                  _layer_spec((1, D_MODEL), layer),
                  pl.BlockSpec((None, tn, D_MODEL), lambda j, i: (layer, j, 0))],
        out_specs=pl.BlockSpec((tm, tn), lambda j, i: (i, j)),
        out_shape=jax.ShapeDtypeStruct((m, N_PROJ), F32),
        compiler_params=_cparams(("parallel", "parallel")),
        name="in_proj",
    )(x2, nw, w)


def _halo_specs(tl, width, col_block, nt, nrows):
    per = tl // SUBLANES
    last = nrows // SUBLANES - 1
    cur = pl.BlockSpec((tl, width), lambda b, i: (b * nt + i, col_block))
    prev = pl.BlockSpec((SUBLANES, width), lambda b, i: (jnp.maximum((b * nt + i) * per - 1, 0), col_block))
    nxt = pl.BlockSpec((SUBLANES, width), lambda b, i: (jnp.minimum((b * nt + i + 1) * per, last), col_block))
    return cur, prev, nxt


def _fill_halo(scr, cur_ref, prev_ref, next_ref):
    i = pl.program_id(1)
    tl = cur_ref.shape[0]
    scr[0:SUBLANES, :] = jnp.where(i > 0, prev_ref[...], 0.0)
    scr[SUBLANES:SUBLANES + tl, :] = cur_ref[...]
    scr[SUBLANES + tl:2 * SUBLANES + tl, :] = jnp.where(i < pl.num_programs(1) - 1, next_ref[...], 0.0)


def _conv_body(cur_ref, prev_ref, next_ref, w_ref, b_ref, o_ref, scr):
    _fill_halo(scr, cur_ref, prev_ref, next_ref)
    tl = cur_ref.shape[0]
    acc = b_ref[...] + w_ref[0:1, :] * scr[pl.ds(SUBLANES - 2, tl), :]
    for j in range(1, D_CONV):
        acc = acc + w_ref[j:j + 1, :] * scr[pl.ds(SUBLANES - 2 + j, tl), :]
    o_ref[...] = _silu(acc)


def _conv(layer, proj, conv_w8, conv_b, bsz, seq):
    tl = min(512, seq)
    nt = seq // tl
    width = 1024
    cur, prev, nxt = _halo_specs(tl, width, C_XBC // width, nt, bsz * seq)
    return pl.pallas_call(
        _conv_body,
        grid=(bsz, nt),
        in_specs=[cur, prev, nxt, _layer_spec((SUBLANES, width), layer), _layer_spec((1, width), layer)],
        out_specs=pl.BlockSpec((tl, width), lambda b, i: (b * nt + i, 0)),
        out_shape=jax.ShapeDtypeStruct((bsz * seq, width), F32),
        scratch_shapes=[pltpu.VMEM((tl + 2 * SUBLANES, width), F32)],
        compiler_params=_cparams(("parallel", "parallel")),
        name="ssm_conv",
    )(proj, proj, proj, conv_w8, conv_b)


def _ssd_body(reverse, *refs):
    if reverse:
        (xs_ref, bc_ref, dt_ref, dtb_ref, alog_ref, e_ref, yf_ref, z_ref, dsk_ref, nw_ref, o_ref, st_ref) = refs
    else:
        (xs_ref, bc_ref, dt_ref, dtb_ref, alog_ref, e_ref, o_ref, st_ref) = refs
    c = pl.program_id(0)

    @pl.when(c == 0)
    def _():
        st_ref[...] = jnp.zeros_like(st_ref)

    nb, t = xs_ref.shape[0], xs_ref.shape[1]
    npair = BRANCH // LANES
    row = lax.broadcasted_iota(jnp.int32, (t, t), 0)
    col = lax.broadcasted_iota(jnp.int32, (t, t), 1)
    tri = (col >= row) if reverse else (col <= row)
    tri_bf = jnp.where(tri, 1.0, 0.0).astype(BF16)
    sel = e_ref[...]
    lane = lax.broadcasted_iota(jnp.int32, (t, LANES), 1)
    first = lane < HEAD64
    ofs = SSM_HEADS if reverse else 0
    pre = []
    for bi in range(nb):
        dt_all = _softplus(dt_ref[bi] + dtb_ref[...])
        a_all = dt_all * (-jnp.exp(alog_ref[...]))
        cum = _dot_sel_l(tri_bf, a_all)
        cum_e = _dot_sel_r(cum, sel)
        xs = xs_ref[bi]
        bc = bc_ref[bi]
        pre.append(dict(cum=cum, cum_t=cum.T, cum_e=cum_e, tot_e=cum_e[0:1, :] if reverse else cum_e[t - 1:t, :],
                        xs=xs, xdt=xs * _dot_sel_r(dt_all, sel),
                        bm=[bc[:, g * SSM_STATE:(g + 1) * SSM_STATE] for g in range(2)],
                        cm=[bc[:, 256 + g * SSM_STATE:256 + (g + 1) * SSM_STATE].astype(BF16) for g in range(2)]))
    cb = {(bi, g): _dot_nt(pre[bi]["cm"][g], pre[bi]["bm"][g].astype(BF16))
          for bi in range(nb) for g in range(2)}
    chains = [(bi, p) for bi in range(nb) for p in range(npair)]
    halves = {}
    for bi, p in chains:
        d = pre[bi]
        xdt_bf = d["xdt"][:, p * LANES:(p + 1) * LANES].astype(BF16)
        for hh in range(2):
            ci = ofs + 2 * p + hh
            dec = jnp.exp(jnp.where(tri, d["cum"][:, ci:ci + 1] - d["cum_t"][ci:ci + 1, :], -jnp.inf))
            halves[bi, p, hh] = _dot((cb[bi, p // 2] * dec).astype(BF16), xdt_bf)
    states = [st_ref[bi * npair + p] for bi, p in chains]
    y_off = [_dot(pre[bi]["cm"][p // 2], s.astype(BF16)) for (bi, p), s in zip(chains, states)]
    loc = []
    for bi, p in chains:
        d = pre[bi]
        sl = slice(p * LANES, (p + 1) * LANES)
        loc.append(_dot(d["bm"][p // 2].T.astype(BF16),
                        (d["xdt"][:, sl] * jnp.exp(d["tot_e"][:, sl] - d["cum_e"][:, sl])).astype(BF16)))
    ys = {}
    for n, (bi, p) in enumerate(chains):
        d = pre[bi]
        sl = slice(p * LANES, (p + 1) * LANES)
        st_ref[bi * npair + p] = states[n] * jnp.exp(d["tot_e"][:, sl]) + loc[n]
        ys[bi, p] = (jnp.where(first, halves[bi, p, 0], halves[bi, p, 1]) + y_off[n] * jnp.exp(d["cum_e"][:, sl]))
    for bi in range(nb):
        y = jnp.concatenate([ys[bi, p] for p in range(npair)], axis=1)
        if not reverse:
            o_ref[bi] = y
            continue
        y = y + yf_ref[bi] + pre[bi]["xs"] * dsk_ref[...]
        yg = y * _silu(z_ref[bi])
        nw = nw_ref[...]
        half = BRANCH // 2
        o_ref[bi] = jnp.concatenate(
            [_rms(yg[:, g * half:(g + 1) * half], nw[:, g * half:(g + 1) * half]) for g in range(2)],
            axis=1).astype(o_ref.dtype)


def _ssd(layer, reverse, xbc, proj, dt_bias, a_log, sel, extra, bsz, seq):
    t = min(SSD_CHUNK, seq)
    nc = seq // t
    rmap = (lambda c: nc - 1 - c) if reverse else (lambda c: c)
    as3d = lambda a: a.reshape(bsz, seq, a.shape[-1])
    tile = lambda width, cb: pl.BlockSpec((bsz, t, width), lambda c: (0, rmap(c), cb))
    in_specs = [tile(BRANCH, 0), tile(BRANCH, 1), tile(LANES, C_DT // LANES),
                _layer_spec((1, LANES), layer),
                _layer_spec((1, LANES), layer),
                pl.BlockSpec((LANES, BRANCH), lambda c: (0, 0))]
    args = [as3d(xbc), as3d(xbc), as3d(proj), dt_bias, a_log, sel]
    if reverse:
        y_f, d_skip, norm_w = extra
        in_specs += [tile(BRANCH, 0), tile(BRANCH, C_Z // BRANCH),
                     _layer_spec((1, BRANCH), layer),
                     _layer_spec((1, BRANCH), layer)]
        args += [as3d(y_f), as3d(proj), d_skip, norm_w]
    out = pl.pallas_call(
        functools.partial(_ssd_body, reverse),
        grid=(nc,),
        in_specs=in_specs,
        out_specs=tile(BRANCH, 0),
        out_shape=jax.ShapeDtypeStruct((bsz, seq, BRANCH), BF16 if reverse else F32),
        scratch_shapes=[pltpu.VMEM((bsz * (BRANCH // LANES), SSM_STATE, LANES), F32)],
        compiler_params=_cparams(("arbitrary",)),
        name="ssd_bwd" if reverse else "ssd_fwd",
    )(*args)
    return out.reshape(bsz * seq, BRANCH)


def _rwkv_operands(d, u0, u_nbr, wa0, wa_nbr, mu_ref, muw_ref, w0_ref, w2_ref, a0_ref, a2_ref, kk_ref, ka_ref, rk_ref,
                   blk):
    def head_sum(x):
        return jnp.concatenate(
            [_dot_sel_r2(x[:, j * LANES:(j + 1) * LANES], blk) for j in range(BRANCH // LANES)], axis=1)

    u = u0 + (u_nbr - u0) * mu_ref[d:d + 1, :]
    wa = wa0 + (wa_nbr - wa0) * muw_ref[d:d + 1, :]
    r, k, v = u[:, 0:BRANCH], u[:, BRANCH:2 * BRANCH], u[:, 2 * BRANCH:3 * BRANCH]
    wlin = w0_ref[d:d + 1, :] + _dot_hp(jnp.tanh(wa), w2_ref[d])
    w = -_softplus(-wlin) - 0.5
    lw = -jnp.exp(w)
    eta = jax.nn.sigmoid(a0_ref[...] + _dot_hp(wa, a2_ref[...]))
    kk = k * kk_ref[...]
    kk = kk / jnp.maximum(jnp.sqrt(head_sum(kk * kk)), 1e-12)
    k2 = k * (1.0 + (eta - 1.0) * ka_ref[...])
    bonus = head_sum(r * k2 * rk_ref[...]) * v
    return r, k2, v, lw, -kk, kk * eta, bonus


def _wkv_body(reverse, *refs):
    (u_ref, uh_ref, wa_ref, wah_ref, mu_ref, muw_ref, w0_ref, w2_ref, a0_ref, a2_ref, kk_ref, ka_ref, rk_ref,
     blk_ref) = refs[:14]
    if reverse:
        yf_ref, bonusf_ref, g_ref, lnw_ref, lnb_ref, o_ref, s_ref, scr, wscr = refs[14:]
    else:
        o_ref, bonus_ref, s_ref, scr, wscr = refs[14:]
    ci = pl.program_id(0)

    @pl.when(ci == 0)
    def _():
        s_ref[...] = jnp.zeros_like(s_ref)

    nb, rows = u_ref.shape[0], u_ref.shape[1]
    direction = 1 if reverse else 0
    halo_at = SUBLANES + rows if reverse else 0
    nbr_at = SUBLANES + 1 if reverse else SUBLANES - 1
    blk = blk_ref[...]
    ops = []
    for bi in range(nb):
        scr[bi, halo_at:halo_at + SUBLANES, :] = jnp.where(ci > 0, uh_ref[bi], 0.0)
        wscr[bi, halo_at:halo_at + SUBLANES, :] = jnp.where(ci > 0, wah_ref[bi], 0.0)
        scr[bi, SUBLANES:SUBLANES + rows, :] = u_ref[bi]
        wscr[bi, SUBLANES:SUBLANES + rows, :] = wa_ref[bi]
        ops.append(_rwkv_operands(direction, u_ref[bi], scr[bi, pl.ds(nbr_at, rows), :], wa_ref[bi],
                                  wscr[bi, pl.ds(nbr_at, rows), :], mu_ref, muw_ref, w0_ref, w2_ref, a0_ref, a2_ref,
                                  kk_ref, ka_ref, rk_ref, blk))
    c = min(WKV_CHUNK, rows)
    nsub = rows // c
    c2 = 2 * c
    npair = BRANCH // LANES
    row = lax.broadcasted_iota(jnp.int32, (c, c), 0)
    col = lax.broadcasted_iota(jnp.int32, (c, c), 1)
    tri_bf = jnp.where((col >= row) if reverse else (col <= row), 1.0, 0.0).astype(BF16)
    lane = lax.broadcasted_iota(jnp.int32, (c, LANES), 1)
    first = lane < HEAD64
    row2 = lax.broadcasted_iota(jnp.int32, (c2, c2), 0)
    col2 = lax.broadcasted_iota(jnp.int32, (c2, c2), 1)
    same = jnp.where(row2 >= c, 1, 0) == jnp.where(col2 >= c, 1, 0)
    if reverse:
        strict, incl = same & (col2 > row2), same & (col2 >= row2)
    else:
        strict, incl = same & (col2 < row2), same & (col2 <= row2)
    diag = row2 == col2
    eye2 = jnp.where(diag, 1.0, 0.0)
    zero_blk = jnp.zeros((c2, LANES), F32)
    squarings = max(1, int(math.ceil(math.log2(c))) - 1)

    def stack(x):
        return jnp.concatenate([jnp.where(first, x, 0.0), jnp.where(first, 0.0, x)], axis=0)

    subs = list(range(nsub))[::-1] if reverse else list(range(nsub))
    chains = [(bi, sub, p) for sub in subs for bi in range(nb) for p in range(npair)]
    pre = {}
    for bi, sub in [(bi, sub) for sub in subs for bi in range(nb)]:
        rs = slice(sub * c, (sub + 1) * c)
        r, k, v, lw, a, b = (x[rs, :] for x in ops[bi][:6])
        cum = _dot_sel_l(tri_bf, lw)
        cum_prev = cum - lw
        tot = cum[0:1, :] if reverse else cum[c - 1:c, :]
        mid = cum[c // 2:c // 2 + 1, :]
        e_in = jnp.exp(mid - cum)
        e_end = jnp.exp(tot - cum)
        pre[bi, sub] = dict(am=a * jnp.exp(cum_prev - mid), bm=b * e_in, km=k * e_in, rm=r * jnp.exp(cum - mid),
                            a0=a * jnp.exp(cum_prev), r0=r * jnp.exp(cum), bend=b * e_end, kend=k * e_end,
                            gdiag=jnp.exp(tot), v=v)

    def part(name, ch):
        bi, sub, p = ch
        return pre[bi, sub][name][:, p * LANES:(p + 1) * LANES]

    v_s = [stack(part("v", ch)) for ch in chains]
    prod = [_dot_nt(jnp.concatenate([stack(part("am", ch)), stack(part("rm", ch))], axis=0).astype(BF16),
                    jnp.concatenate([stack(part("bm", ch)), stack(part("km", ch))], axis=0).astype(BF16))
            for ch in chains]
    l_ab = [jnp.where(strict, x[0:c2, 0:c2], 0.0) for x in prod]
    l_ak = [jnp.where(strict, x[0:c2, c2:2 * c2], 0.0) for x in prod]
    a_rb = [jnp.where(incl, x[c2:2 * c2, 0:c2], 0.0) for x in prod]
    a_rk = [jnp.where(incl, x[c2:2 * c2, c2:2 * c2], 0.0) for x in prod]
    tinv = [eye2 + x for x in l_ab]
    pw = [x.astype(BF16) for x in l_ab]
    pw = [_dot(x, x).astype(BF16) for x in pw]
    lakv = [_dot1(x, y) for x, y in zip(l_ak, v_s)]
    for i in range(squarings):
        if i == squarings - 1:
            tinv = [t + _dot(t.astype(BF16), x) for t, x in zip(tinv, pw)]
        else:
            xs = [_dot(jnp.concatenate([t.astype(BF16), x], axis=0), x) for t, x in zip(tinv, pw)]
            tinv = [t + x[0:c2, :] for t, x in zip(tinv, xs)]
            pw = [x[c2:2 * c2, :].astype(BF16) for x in xs]
    au = [_dot(t.astype(BF16), jnp.concatenate([stack(part("a0", ch)), x], axis=1).astype(BF16))
          for t, x, ch in zip(tinv, lakv, chains)]
    big = []
    for n, ch in enumerate(chains):
        ends_t = jnp.concatenate([stack(part("bend", ch)), stack(part("kend", ch))], axis=0).T
        lhs3 = jnp.concatenate([jnp.concatenate([a_rb[n], a_rk[n]], axis=1), ends_t], axis=0).astype(BF16)
        rhs3 = jnp.concatenate([au[n], jnp.concatenate([zero_blk, v_s[n]], axis=1)], axis=0).astype(BF16)
        big.append(_dot(lhs3, rhs3))
    lhs4 = []
    for n, ch in enumerate(chains):
        r_hat = stack(part("r0", ch)) + big[n][0:c2, 0:LANES]
        gmat = jnp.where(diag, part("gdiag", ch), 0.0) + big[n][c2:c2 + LANES, 0:LANES]
        g_hi, g_lo = _split2(gmat)
        lhs4.append((jnp.concatenate([r_hat.astype(BF16), g_hi], axis=0), g_lo))
    ys = {}
    per_sub = nb * npair
    for si, sub in enumerate(subs):
        idx = range(si * per_sub, (si + 1) * per_sub)
        st = [_split2(s_ref[n - si * per_sub]) for n in idx]
        res = [_dot(lhs4[n][0], s_hi) for n, (s_hi, s_lo) in zip(idx, st)]
        fine = [_dot(lhs4[n][0][c2:c2 + LANES, :], s_lo) + _dot(lhs4[n][1], s_hi) for n, (s_hi, s_lo) in zip(idx, st)]
        for n, x, f in zip(idx, res, fine):
            s_ref[n - si * per_sub] = x[c2:c2 + LANES, :] + f + big[n][c2:c2 + LANES, LANES:2 * LANES]
            y_s = x[0:c2, :] + big[n][0:c2, LANES:2 * LANES]
            ys[chains[n]] = y_s[0:c, :] + y_s[c:c2, :]
    if not reverse:
        for bi in range(nb):
            bonus_ref[bi] = ops[bi][6]
    for bi, sub in [(bi, sub) for sub in subs for bi in range(nb)]:
        rs = slice(sub * c, (sub + 1) * c)
        y = jnp.concatenate([ys[bi, sub, p] for p in range(npair)], axis=1)
        if not reverse:
            o_ref[bi, rs, :] = y
            continue
        wkv = y + yf_ref[bi, rs, :]
        inv_n = 1.0 / HEAD64

        def head_mean(x):
            return jnp.concatenate(
                [_dot_sel_r2(x[:, j * LANES:(j + 1) * LANES], blk) for j in range(npair)], axis=1) * inv_n

        cen = wkv - head_mean(wkv)
        gn = cen * lax.rsqrt(head_mean(cen * cen) + RWKV_GN_EPS) * lnw_ref[...] + lnb_ref[...]
        bonus = bonusf_ref[bi, rs, :] + ops[bi][6][rs, :]
        o_ref[bi, rs, :] = ((gn + bonus) * _silu(g_ref[bi, rs, :])).astype(o_ref.dtype)


def _wkv(layer, reverse, proj, prm, extra, bsz, seq):
    c = min(WKV_CHUNK * WKV_SUB, seq)
    nc = seq // c
    npair = BRANCH // LANES
    wide = 3 * BRANCH
    per = c // SUBLANES
    rmap = (lambda ci: nc - 1 - ci) if reverse else (lambda ci: ci)
    if reverse:
        hmap = lambda ci: jnp.minimum((rmap(ci) + 1) * per, seq // SUBLANES - 1)
    else:
        hmap = lambda ci: jnp.maximum(rmap(ci) * per - 1, 0)
    as3d = lambda t: t.reshape(bsz, seq, t.shape[-1])
    proj3 = as3d(proj)
    tile = pl.BlockSpec((bsz, c, BRANCH), lambda ci: (0, rmap(ci), 0))
    full = lambda shape: _layer_spec(shape, layer)
    in_specs = [pl.BlockSpec((bsz, c, wide), lambda ci: (0, rmap(ci), C_RKV // wide)),
                pl.BlockSpec((bsz, SUBLANES, wide), lambda ci: (0, hmap(ci), C_RKV // wide)),
                pl.BlockSpec((bsz, c, LANES), lambda ci: (0, rmap(ci), C_WA // LANES)),
                pl.BlockSpec((bsz, SUBLANES, LANES), lambda ci: (0, hmap(ci), C_WA // LANES)),
                full((2, wide)), full((2, LANES)), full((2, BRANCH)), full((2, LANES, BRANCH)),
                full((1, BRANCH)), full((LANES, BRANCH)), full((1, BRANCH)), full((1, BRANCH)), full((1, BRANCH)),
                pl.BlockSpec((LANES, LANES), lambda ci: (0, 0))]
    args = [proj3, proj3, proj3, proj3, prm["mu_rkv"], prm["mu_wa"], prm["w0"], prm["w2"], prm["a0"], prm["a2"],
            prm["k_k"], prm["k_a"], prm["r_k"], prm["blk"]]
    out_sds = jax.ShapeDtypeStruct((bsz, seq, BRANCH), F32)
    if reverse:
        y_f, bonus_f, ln_w, ln_b = extra
        in_specs += [tile, tile,
                     pl.BlockSpec((bsz, c, BRANCH), lambda ci: (0, rmap(ci), C_RG // BRANCH)),
                     full((1, BRANCH)), full((1, BRANCH))]
        args += [y_f, bonus_f, proj3, ln_w, ln_b]
        out_specs, out_shape = tile, jax.ShapeDtypeStruct((bsz, seq, BRANCH), BF16)
    else:
        out_specs, out_shape = [tile, tile], [out_sds, out_sds]
    return pl.pallas_call(
        functools.partial(_wkv_body, reverse),
        grid=(nc,),
        in_specs=in_specs,
        out_specs=out_specs,
        out_shape=out_shape,
        scratch_shapes=[pltpu.VMEM((bsz * npair, LANES, LANES), F32),
                        pltpu.VMEM((bsz, c + 2 * SUBLANES, wide), F32),
                        pltpu.VMEM((bsz, c + 2 * SUBLANES, LANES), F32)],
        compiler_params=_cparams(("arbitrary",)),
        name="wkv_bwd" if reverse else "wkv_fwd",
    )(*args)


def _rope(x, cos, sin_signed):
    blocks = []
    for j in range(x.shape[1] // LANES):
        xb = x[:, j * LANES:(j + 1) * LANES]
        blocks.append(xb * cos + pltpu.roll(xb, LANES // 2, 1) * sin_signed)
    return jnp.concatenate(blocks, axis=1) if len(blocks) > 1 else blocks[0]


def _attn_prep_body(dq_ref, dk_ref, dv_ref, gq_ref, gk_ref, gv_ref, cd_ref, sd_ref, cg_ref, sg_ref, qw_ref, kw_ref,
                    q1_ref, q2_ref, dko_ref, dvo_ref, gqo_ref, gko_ref, gvo_ref):
    cd, sd, cg, sg = cd_ref[...], sd_ref[...], cg_ref[...], sg_ref[...]
    q = _rope(dq_ref[...], cd, sd) * (HEAD64 ** -0.5 * LOG2E)
    lane = lax.broadcasted_iota(jnp.int32, q.shape, 1)
    comp0 = (lane & (HEAD64 // 2)) == 0
    q1_ref[...] = jnp.where(comp0, q, 0.0).astype(BF16)
    q2_ref[...] = jnp.where(comp0, 0.0, q).astype(BF16)
    dko_ref[...] = _rope(dk_ref[...], cd, sd).astype(BF16)

    def store_vt(v_ref, o_ref):
        vt = v_ref[...].T
        tl = vt.shape[1]
        for h in range(vt.shape[0] // LANES):
            o_ref[0, h, 0:LANES, :] = vt[h * LANES:(h + 1) * LANES, :].astype(BF16)
            o_ref[0, h, LANES:VT_ROWS, :] = jnp.ones((VT_ROWS - LANES, tl), BF16)

    store_vt(dv_ref, dvo_ref)

    def norm_heads(x, w):
        return jnp.concatenate(
            [_rms(x[:, j * LANES:(j + 1) * LANES], w) for j in range(x.shape[1] // LANES)], axis=1)

    gq = _rope(norm_heads(gq_ref[...], qw_ref[...]), cg, sg) * (LANES ** -0.5 * LOG2E)
    gqo_ref[...] = gq.astype(BF16)
    gko_ref[...] = _rope(norm_heads(gk_ref[...], kw_ref[...]), cg, sg).astype(BF16)
    store_vt(gv_ref, gvo_ref)


def _attn_prep(layer, proj, tabs, qw, kw, bsz, seq):
    tl = min(256, seq)
    nt = seq // tl
    m = bsz * seq
    colspec = lambda width, off: pl.BlockSpec((tl, width), lambda b, i: (b * nt + i, off // width))
    tab = pl.BlockSpec((tl, LANES), lambda b, i: (i, 0))
    wspec = _layer_spec((1, LANES), layer)
    o512 = pl.BlockSpec((tl, BRANCH), lambda b, i: (b * nt + i, 0))
    o256 = pl.BlockSpec((tl, 256), lambda b, i: (b * nt + i, 0))
    s512 = jax.ShapeDtypeStruct((m, BRANCH), BF16)
    s256 = jax.ShapeDtypeStruct((m, 256), BF16)
    vt_spec = lambda heads: pl.BlockSpec((1, heads, VT_ROWS, tl), lambda b, i: (b, 0, 0, i))
    vt_shape = lambda heads: jax.ShapeDtypeStruct((bsz, heads, VT_ROWS, seq), BF16)
    return pl.pallas_call(
        _attn_prep_body,
        grid=(bsz, nt),
        in_specs=[colspec(BRANCH, C_DQ), colspec(BRANCH, C_DK), colspec(BRANCH, C_DV), colspec(BRANCH, C_GQ),
                  colspec(256, C_GK), colspec(256, C_GV), tab, tab, tab, tab, wspec, wspec],
        out_specs=[o512, o512, o512, vt_spec(4), o512, o256, vt_spec(2)],
        out_shape=[s512, s512, s512, vt_shape(4), s512, s256, vt_shape(2)],
        compiler_params=_cparams(("parallel", "parallel")),
        name="attn_prep",
    )(proj, proj, proj, proj, proj, proj, *tabs, qw, kw)


def _flash_body(diff, scale_out, ck, *refs):
    if diff:
        qa_ref, qb_ref, k_ref, vt_ref, gate_ref, lam_ref, nw_ref, o_ref, s_scr, p_scr, acc_scr = refs
    else:
        qa_ref, qb_ref, k_ref, vt_ref, gate_ref, o_ref, s_scr, p_scr, acc_scr = refs
    tq = qa_ref.shape[0]
    q = jnp.concatenate([qa_ref[...], qb_ref[...]], axis=0)
    nchunk = k_ref.shape[0] // ck

    def scores(j, slot):
        off = pl.multiple_of(j * ck, ck)
        s = _dot_nt(k_ref[pl.ds(off, ck), :], q)
        s_scr[slot] = s
        return jnp.max(s, axis=0, keepdims=True)

    def pv(j, slot, alpha):
        off = pl.multiple_of(j * ck, ck)
        acc_scr[...] = alpha * acc_scr[...] + _dot(vt_ref[0, 0, :, pl.ds(off, ck)], p_scr[slot])

    def step(j, slot, m_prev, cmax, alpha_prev):
        cmax_next = scores(jnp.minimum(j + 1, nchunk - 1), 1 - slot)
        pv(jnp.maximum(j - 1, 0), 1 - slot, alpha_prev)
        m_new = jnp.maximum(m_prev, cmax)
        p_scr[slot] = jnp.exp2(s_scr[slot] - m_new).astype(BF16)
        return m_new, cmax_next, jnp.exp2(m_prev - m_new)

    def body(i, carry):
        m, cmax, alpha = carry
        m, cmax, alpha = step(2 * i, 0, m, cmax, alpha)
        return step(2 * i + 1, 1, m, cmax, alpha)

    acc_scr[...] = jnp.zeros_like(acc_scr)
    p_scr[1] = jnp.zeros(p_scr.shape[1:], BF16)
    init = (jnp.full((1, 2 * tq), -jnp.inf, F32), scores(0, 0), jnp.ones((1, 2 * tq), F32))
    _, _, alpha_last = lax.fori_loop(0, nchunk // 2, body, init)
    pv(nchunk - 1, 1, alpha_last)
    acc = acc_scr[...]
    o = (acc[0:LANES, :] / acc[LANES:LANES + 1, :]).T
    oa, ob = o[0:tq, :], o[tq:2 * tq, :]
    if diff:
        lp = lam_ref[...]
        lam = (jnp.exp(jnp.sum(lp[0:1, :] * lp[1:2, :], axis=1, keepdims=True))
               - jnp.exp(jnp.sum(lp[2:3, :] * lp[3:4, :], axis=1, keepdims=True)) + (1.0 - scale_out))
        out = _rms(oa - lam * ob, nw_ref[...]) * scale_out
    else:
        out = jnp.concatenate([oa, ob], axis=1)
    o_ref[...] = (out * _silu(gate_ref[...])).astype(o_ref.dtype)


def _flash(layer, diff, qa, qb, k, vt, proj, extra, lambda_init, bsz, seq):
    tq = min(FLASH_TQ, seq)
    ck = min(FLASH_CK, seq // 2)
    assert seq % (2 * ck) == 0 and seq % tq == 0
    nq = seq // tq
    groups = k.shape[1] // LANES
    out_w = LANES if diff else 2 * LANES
    gate_col = (C_DG if diff else C_GG) // out_w
    qa_spec = pl.BlockSpec((tq, LANES), lambda b, g, i: (b * nq + i, g if diff else 2 * g))
    qb_spec = pl.BlockSpec((tq, LANES), lambda b, g, i: (b * nq + i, g if diff else 2 * g + 1))
    k_spec = pl.BlockSpec((seq, LANES), lambda b, g, i: (b, g))
    vt_spec = pl.BlockSpec((1, 1, VT_ROWS, seq), lambda b, g, i: (b, g, 0, 0))
    gate_spec = pl.BlockSpec((tq, out_w), lambda b, g, i: (b * nq + i, gate_col + g))
    in_specs = [qa_spec, qb_spec, k_spec, vt_spec, gate_spec]
    args = [qa, qb, k, vt, proj]
    if diff:
        in_specs += [_layer_spec((4, HEAD64), layer), _layer_spec((1, LANES), layer)]
        args += list(extra)
    out_spec = pl.BlockSpec((tq, out_w), lambda b, g, i: (b * nq + i, g))
    return pl.pallas_call(
        functools.partial(_flash_body, diff, 1.0 - lambda_init, ck),
        grid=(bsz, groups, nq),
        in_specs=in_specs,
        out_specs=out_spec,
        out_shape=jax.ShapeDtypeStruct((bsz * seq, BRANCH), BF16),
        scratch_shapes=[pltpu.VMEM((2, ck, 2 * tq), F32), pltpu.VMEM((2, ck, 2 * tq), BF16),
                        pltpu.VMEM((VT_ROWS, 2 * tq), F32)],
        compiler_params=_cparams(("parallel", "parallel", "parallel")),
        name="diff_attn" if diff else "gqa_attn",
    )(*args)


def _outproj_body(x_ref, ya_ref, yb_ref, yc_ref, yd_ref, w_ref, pw_ref, o_ref):
    mix = (_dot(ya_ref[...], w_ref[0]) + _dot(yb_ref[...], w_ref[1])
           + _dot(yc_ref[...], w_ref[2]) + _dot(yd_ref[...], w_ref[3]))
    o_ref[...] = x_ref[...] + _rms(mix, pw_ref[...])


def _out_proj(layer, x2, y_a, y_b, y_c, y_d, w4, pw):
    m = x2.shape[0]
    tm = min(512, m)
    rows = lambda width: pl.BlockSpec((tm, width), lambda i: (i, 0))
    return pl.pallas_call(
        _outproj_body,
        grid=(m // tm,),
        in_specs=[rows(D_MODEL), rows(BRANCH), rows(BRANCH), rows(BRANCH), rows(BRANCH),
                  _layer_spec((4, BRANCH, D_MODEL), layer), _layer_spec((1, D_MODEL), layer)],
        out_specs=rows(D_MODEL),
        out_shape=jax.ShapeDtypeStruct((m, D_MODEL), F32),
        compiler_params=_cparams(("parallel",)),
        name="out_proj",
    )(x2, y_a, y_b, y_c, y_d, w4, pw)


def _rotary_cols(start, width):
    quarter = HEAD64 // 2
    return tuple((start + blk + q * quarter, quarter) for blk in range(0, width, LANES) for q in (0, 2, 1, 3))


_W_IN_PIECES = (((512, 1024), (0, 512), (1552, 1536), (3216, 512))
                + _rotary_cols(3728, 1024) + ((4752, 1024),)
                + _rotary_cols(5776, 768) + ((6544, 768),)
                + ((3088, 128), (1536, 16)))


def _reorder_body(w_ref, o_ref):
    dst = 0
    for src, width in _W_IN_PIECES:
        o_ref[dst:dst + width, :] = w_ref[src:src + width, :].astype(BF16)
        dst += width
    o_ref[dst:N_PROJ, :] = jnp.zeros((N_PROJ - dst, o_ref.shape[1]), BF16)


def _reorder_w_in(w):
    w_t = jnp.swapaxes(w, 1, 2)
    depth, cols, rows = w_t.shape
    tc = 256
    return pl.pallas_call(
        _reorder_body,
        grid=(depth, rows // tc),
        in_specs=[pl.BlockSpec((None, cols, tc), lambda l, i: (l, 0, i))],
        out_specs=pl.BlockSpec((None, N_PROJ, tc), lambda l, i: (l, 0, i)),
        out_shape=jax.ShapeDtypeStruct((depth, N_PROJ, rows), BF16),
        compiler_params=_cparams(("parallel", "parallel")),
        name="w_in_layout",
    )(w_t)


def _rope_tables(seq):
    half = HEAD64 // 2
    inv = (np.float32(ROPE_THETA) ** (-np.arange(half, dtype=np.float32) / np.float32(half))).astype(np.float32)
    t = np.arange(seq, dtype=np.int32)

    def tab(pos):
        ang = (pos.astype(np.float32)[:, None] * inv[None, :]).astype(np.float64)
        return np.cos(ang).astype(np.float32), np.sin(ang).astype(np.float32)

    def lanes(g1, g2):
        (c1, s1), (c2, s2) = g1, g2
        return np.concatenate([c1, c2, c1, c2], axis=1), np.concatenate([-s1, -s2, s1, s2], axis=1)

    p, r, c = tab(t), tab(t // GRID_W), tab(t % GRID_W)
    return tuple(jnp.asarray(a) for a in lanes(p, p) + lanes(r, c))


def _pad_lanes(v, width=LANES):
    v = v.reshape(v.shape[0], 1, -1)
    return jnp.pad(v, ((0, 0), (0, 0), (0, width - v.shape[2])))


def kernel(x, pre_norm_w, post_norm_w, w_in, w_out, conv_w, conv_b, ssm_a_log, ssm_dt_bias, ssm_d, ssm_norm_w,
           rwkv_mu, rwkv_w0, rwkv_w2, rwkv_a0, rwkv_a2, rwkv_k_k, rwkv_k_a, rwkv_r_k, rwkv_ln_w, rwkv_ln_b,
           diff_lambda, diff_norm_w, gqa_q_norm_w, gqa_k_norm_w):
    bsz, seq, _ = x.shape
    depth = w_in.shape[0]
    m = bsz * seq
    x2 = x.reshape(m, D_MODEL)
    tabs = _rope_tables(seq)
    lane_head = jnp.arange(LANES) // HEAD64
    blk = (lane_head[:, None] == lane_head[None, :]).astype(BF16)
    head_of_lane = jnp.arange(BRANCH) // HEAD64
    sels = [(jnp.arange(LANES)[:, None] == (d * SSM_HEADS + head_of_lane)[None, :]).astype(BF16) for d in range(2)]
    row = lambda v: v.reshape(depth, 1, -1)
    w_in_r = _reorder_w_in(w_in)
    w_out4 = w_out.reshape(depth, 4, BRANCH, D_MODEL).astype(BF16)
    pre_w, post_w = row(pre_norm_w), row(post_norm_w)
    conv_w8 = jnp.pad(conv_w, ((0, 0), (0, SUBLANES - D_CONV), (0, 0)))
    conv_b1 = row(conv_b)
    dt_bias, a_log = _pad_lanes(ssm_dt_bias), _pad_lanes(ssm_a_log)
    d_skip = row(jnp.repeat(ssm_d, HEAD64, axis=-1))
    ssm_nw = row(ssm_norm_w)
    prm = {
        "mu_rkv": rwkv_mu[:, :, :3 * BRANCH], "mu_wa": rwkv_mu[:, :, 3 * BRANCH:], "w0": rwkv_w0,
        "w2": jnp.pad(rwkv_w2, ((0, 0), (0, 0), (0, LANES - RWKV_RANK), (0, 0))),
        "a0": row(rwkv_a0), "a2": jnp.pad(rwkv_a2, ((0, 0), (LANES - RWKV_RANK, 0), (0, 0))),
        "k_k": row(rwkv_k_k), "k_a": row(rwkv_k_a), "r_k": row(rwkv_r_k), "blk": blk,
    }
    ln_w, ln_b = row(rwkv_ln_w), row(rwkv_ln_b)
    rotary_order = lambda v: row(jnp.swapaxes(v.reshape(depth, 2, 2, HEAD64 // 2), 1, 2))
    diff_nw, gqa_qw, gqa_kw = row(diff_norm_w), rotary_order(gqa_q_norm_w), rotary_order(gqa_k_norm_w)
    for i in range(depth):
        lambda_init = 0.8 - 0.6 * math.exp(-0.3 * i)
        proj = _in_proj(i, x2, pre_w, w_in_r)
        xbc = _conv(i, proj, conv_w8, conv_b1, bsz, seq)
        y_f = _ssd(i, False, xbc, proj, dt_bias, a_log, sels[0], None, bsz, seq)
        y_a = _ssd(i, True, xbc, proj, dt_bias, a_log, sels[1], (y_f, d_skip, ssm_nw), bsz, seq)
        wkv_f, bonus_f = _wkv(i, False, proj, prm, None, bsz, seq)
        y_b = _wkv(i, True, proj, prm, (wkv_f, bonus_f, ln_w, ln_b), bsz, seq).reshape(m, BRANCH)
        q1, q2, dk, dv, gq, gk, gv = _attn_prep(i, proj, tabs, gqa_qw, gqa_kw, bsz, seq)
        y_c = _flash(i, True, q1, q2, dk, dv, proj, (diff_lambda, diff_nw), lambda_init, bsz, seq)
        y_d = _flash(i, False, gq, gq, gk, gv, proj, None, lambda_init, bsz, seq)
        x2 = _out_proj(i, x2, y_a, y_b, y_c, y_d, w_out4, post_w)
    return x2.reshape(bsz, seq, D_MODEL)
```

```python
import functools
import math

import jax
import jax.numpy as jnp
import numpy as np
from jax import lax
from jax.experimental import pallas as pl
from jax.experimental.pallas import tpu as pltpu

F32 = jnp.float32
BF16 = jnp.bfloat16

D_MODEL = 1024
GRID_W = 64
ROPE_THETA = 10000.0
NORM_EPS = 1e-6
BRANCH = 512
SSM_HEADS = 8
SSM_STATE = 128
D_CONV = 5
RWKV_RANK = 64
RWKV_GN_EPS = 64e-5
HEAD64 = 64
LANES = 128
SUBLANES = 8

C_XBC, C_Z = 0, 1024
C_RKV, C_RG = 1536, 3072
C_DQ, C_DK, C_DV, C_DG = 3584, 4096, 4608, 5120
C_GQ, C_GK, C_GV, C_GG = 5632, 6144, 6400, 6656
C_WA, C_DT = 7168, 7296
N_PROJ = 7424

SSD_CHUNK = 256
WKV_CHUNK = 64
WKV_SUB = 4
FLASH_TQ = 2048
FLASH_CK = 512
VT_ROWS = LANES + 16
LOG2E = 1.4426950408889634
VMEM_LIMIT = 48 * 1024 * 1024


def _cparams(sem):
    return pltpu.CompilerParams(dimension_semantics=sem, vmem_limit_bytes=VMEM_LIMIT)


def _layer_spec(shape, layer):
    return pl.BlockSpec((None,) + tuple(shape), lambda *_: (layer,) + (0,) * len(shape))


def _dot(a, b):
    return jnp.dot(a, b, preferred_element_type=F32)


def _dot_nt(a, b):
    return lax.dot_general(a, b, (((1,), (1,)), ((), ())), preferred_element_type=F32)


def _split2(x):
    hi = x.astype(BF16)
    lo = (x - hi.astype(F32)).astype(BF16)
    return hi, lo


def _split3(x):
    hi = x.astype(BF16)
    r1 = x - hi.astype(F32)
    mid = r1.astype(BF16)
    lo = (r1 - mid.astype(F32)).astype(BF16)
    return hi, mid, lo


def _dot_sel_l(sel, x):
    hi, mid, lo = _split3(x)
    return _dot(sel, hi) + _dot(sel, mid) + _dot(sel, lo)


def _dot_sel_r(x, sel):
    hi, mid, lo = _split3(x)
    return _dot(hi, sel) + _dot(mid, sel) + _dot(lo, sel)


def _dot_sel_r2(x, sel):
    hi, lo = _split2(x)
    return _dot(hi, sel) + _dot(lo, sel)


def _dot_hp(a, b):
    ah, al = _split2(a)
    bh, bl = _split2(b)
    return _dot(ah, bh) + _dot(ah, bl) + _dot(al, bh)


def _dot1(a, b):
    return _dot(a.astype(BF16), b.astype(BF16))


def _softplus(x):
    return jnp.maximum(x, 0.0) + jnp.log(1.0 + jnp.exp(-jnp.abs(x)))


def _silu(x):
    return x * jax.nn.sigmoid(x)


def _rms(x, w):
    return x * lax.rsqrt(jnp.mean(x * x, axis=-1, keepdims=True) + NORM_EPS) * w


def _inproj_body(x_ref, nw_ref, w_ref, o_ref):
    h = _rms(x_ref[...], nw_ref[...]).astype(BF16)
    o_ref[...] = _dot_nt(h, w_ref[...])


def _in_proj(layer, x2, nw, w):
    m = x2.shape[0]
    tm, tn = min(512, m), N_PROJ // 2
    return pl.pallas_call(
        _inproj_body,
        grid=(N_PROJ // tn, m // tm),
        in_specs=[pl.BlockSpec((tm, D_MODEL), lambda j, i: (i, 0)),
                  _layer_spec((1, D_MODEL), layer),
                  pl.BlockSpec((None, tn, D_MODEL), lambda j, i: (layer, j, 0))],
        out_specs=pl.BlockSpec((tm, tn), lambda j, i: (i, j)),
        out_shape=jax.ShapeDtypeStruct((m, N_PROJ), F32),
        compiler_params=_cparams(("parallel", "parallel")),
        name="in_proj",
    )(x2, nw, w)


def _halo_specs(tl, width, col_block, nt, nrows):
    per = tl // SUBLANES
    last = nrows // SUBLANES - 1
    cur = pl.BlockSpec((tl, width), lambda b, i: (b * nt + i, col_block))
    prev = pl.BlockSpec((SUBLANES, width), lambda b, i: (jnp.maximum((b * nt + i) * per - 1, 0), col_block))
    nxt = pl.BlockSpec((SUBLANES, width), lambda b, i: (jnp.minimum((b * nt + i + 1) * per, last), col_block))
    return cur, prev, nxt


def _fill_halo(scr, cur_ref, prev_ref, next_ref):
    i = pl.program_id(1)
    tl = cur_ref.shape[0]
    scr[0:SUBLANES, :] = jnp.where(i > 0, prev_ref[...], 0.0)
    scr[SUBLANES:SUBLANES + tl, :] = cur_ref[...]
    scr[SUBLANES + tl:2 * SUBLANES + tl, :] = jnp.where(i < pl.num_programs(1) - 1, next_ref[...], 0.0)


def _conv_body(cur_ref, prev_ref, next_ref, w_ref, b_ref, o_ref, scr):
    _fill_halo(scr, cur_ref, prev_ref, next_ref)
    tl = cur_ref.shape[0]
    acc = b_ref[...] + w_ref[0:1, :] * scr[pl.ds(SUBLANES - 2, tl), :]
    for j in range(1, D_CONV):
        acc = acc + w_ref[j:j + 1, :] * scr[pl.ds(SUBLANES - 2 + j, tl), :]
    o_ref[...] = _silu(acc)


def _conv(layer, proj, conv_w8, conv_b, bsz, seq):
    tl = min(512, seq)
    nt = seq // tl
    width = 1024
    cur, prev, nxt = _halo_specs(tl, width, C_XBC // width, nt, bsz * seq)
    return pl.pallas_call(
        _conv_body,
        grid=(bsz, nt),
        in_specs=[cur, prev, nxt, _layer_spec((SUBLANES, width), layer), _layer_spec((1, width), layer)],
        out_specs=pl.BlockSpec((tl, width), lambda b, i: (b * nt + i, 0)),
        out_shape=jax.ShapeDtypeStruct((bsz * seq, width), F32),
        scratch_shapes=[pltpu.VMEM((tl + 2 * SUBLANES, width), F32)],
        compiler_params=_cparams(("parallel", "parallel")),
        name="ssm_conv",
    )(proj, proj, proj, conv_w8, conv_b)


def _ssd_body(reverse, *refs):
    if reverse:
        (xs_ref, bc_ref, dt_ref, dtb_ref, alog_ref, e_ref, yf_ref, z_ref, dsk_ref, nw_ref, o_ref, st_ref) = refs
    else:
        (xs_ref, bc_ref, dt_ref, dtb_ref, alog_ref, e_ref, o_ref, st_ref) = refs
    c = pl.program_id(0)

    @pl.when(c == 0)
    def _():
        st_ref[...] = jnp.zeros_like(st_ref)

    nb, t = xs_ref.shape[0], xs_ref.shape[1]
    npair = BRANCH // LANES
    row = lax.broadcasted_iota(jnp.int32, (t, t), 0)
    col = lax.broadcasted_iota(jnp.int32, (t, t), 1)
    tri = (col >= row) if reverse else (col <= row)
    tri_bf = jnp.where(tri, 1.0, 0.0).astype(BF16)
    sel = e_ref[...]
    lane = lax.broadcasted_iota(jnp.int32, (t, LANES), 1)
    first = lane < HEAD64
    ofs = SSM_HEADS if reverse else 0
    pre = []
    for bi in range(nb):
        dt_all = _softplus(dt_ref[bi] + dtb_ref[...])
        a_all = dt_all * (-jnp.exp(alog_ref[...]))
        cum = _dot_sel_l(tri_bf, a_all)
        cum_e = _dot_sel_r(cum, sel)
        xs = xs_ref[bi]
        bc = bc_ref[bi]
        pre.append(dict(cum=cum, cum_t=cum.T, cum_e=cum_e, tot_e=cum_e[0:1, :] if reverse else cum_e[t - 1:t, :],
                        xs=xs, xdt=xs * _dot_sel_r(dt_all, sel),
                        bm=[bc[:, g * SSM_STATE:(g + 1) * SSM_STATE] for g in range(2)],
                        cm=[bc[:, 256 + g * SSM_STATE:256 + (g + 1) * SSM_STATE].astype(BF16) for g in range(2)]))
    cb = {(bi, g): _dot_nt(pre[bi]["cm"][g], pre[bi]["bm"][g].astype(BF16))
          for bi in range(nb) for g in range(2)}
    chains = [(bi, p) for bi in range(nb) for p in range(npair)]
    halves = {}
    for bi, p in chains:
        d = pre[bi]
        xdt_bf = d["xdt"][:, p * LANES:(p + 1) * LANES].astype(BF16)
        for hh in range(2):
            ci = ofs + 2 * p + hh
            dec = jnp.exp(jnp.where(tri, d["cum"][:, ci:ci + 1] - d["cum_t"][ci:ci + 1, :], -jnp.inf))
            halves[bi, p, hh] = _dot((cb[bi, p // 2] * dec).astype(BF16), xdt_bf)
    states = [st_ref[bi * npair + p] for bi, p in chains]
    y_off = [_dot(pre[bi]["cm"][p // 2], s.astype(BF16)) for (bi, p), s in zip(chains, states)]
    loc = []
    for bi, p in chains:
        d = pre[bi]
        sl = slice(p * LANES, (p + 1) * LANES)
        loc.append(_dot(d["bm"][p // 2].T.astype(BF16),
                        (d["xdt"][:, sl] * jnp.exp(d["tot_e"][:, sl] - d["cum_e"][:, sl])).astype(BF16)))
    ys = {}
    for n, (bi, p) in enumerate(chains):
        d = pre[bi]
        sl = slice(p * LANES, (p + 1) * LANES)
        st_ref[bi * npair + p] = states[n] * jnp.exp(d["tot_e"][:, sl]) + loc[n]
        ys[bi, p] = (jnp.where(first, halves[bi, p, 0], halves[bi, p, 1]) + y_off[n] * jnp.exp(d["cum_e"][:, sl]))
    for bi in range(nb):
        y = jnp.concatenate([ys[bi, p] for p in range(npair)], axis=1)
        if not reverse:
            o_ref[bi] = y
            continue
        y = y + yf_ref[bi] + pre[bi]["xs"] * dsk_ref[...]
        yg = y * _silu(z_ref[bi])
        nw = nw_ref[...]
        half = BRANCH // 2
        o_ref[bi] = jnp.concatenate(
            [_rms(yg[:, g * half:(g + 1) * half], nw[:, g * half:(g + 1) * half]) for g in range(2)],
            axis=1).astype(o_ref.dtype)


def _ssd(layer, reverse, xbc, proj, dt_bias, a_log, sel, extra, bsz, seq):
    t = min(SSD_CHUNK, seq)
    nc = seq // t
    rmap = (lambda c: nc - 1 - c) if reverse else (lambda c: c)
    as3d = lambda a: a.reshape(bsz, seq, a.shape[-1])
    tile = lambda width, cb: pl.BlockSpec((bsz, t, width), lambda c: (0, rmap(c), cb))
    in_specs = [tile(BRANCH, 0), tile(BRANCH, 1), tile(LANES, C_DT // LANES),
                _layer_spec((1, LANES), layer),
                _layer_spec((1, LANES), layer),
                pl.BlockSpec((LANES, BRANCH), lambda c: (0, 0))]
    args = [as3d(xbc), as3d(xbc), as3d(proj), dt_bias, a_log, sel]
    if reverse:
        y_f, d_skip, norm_w = extra
        in_specs += [tile(BRANCH, 0), tile(BRANCH, C_Z // BRANCH),
                     _layer_spec((1, BRANCH), layer),
                     _layer_spec((1, BRANCH), layer)]
        args += [as3d(y_f), as3d(proj), d_skip, norm_w]
    out = pl.pallas_call(
        functools.partial(_ssd_body, reverse),
        grid=(nc,),
        in_specs=in_specs,
        out_specs=tile(BRANCH, 0),
        out_shape=jax.ShapeDtypeStruct((bsz, seq, BRANCH), BF16 if reverse else F32),
        scratch_shapes=[pltpu.VMEM((bsz * (BRANCH // LANES), SSM_STATE, LANES), F32)],
        compiler_params=_cparams(("arbitrary",)),
        name="ssd_bwd" if reverse else "ssd_fwd",
    )(*args)
    return out.reshape(bsz * seq, BRANCH)


def _rwkv_operands(d, u0, u_nbr, wa0, wa_nbr, mu_ref, muw_ref, w0_ref, w2_ref, a0_ref, a2_ref, kk_ref, ka_ref, rk_ref,
                   blk):
    def head_sum(x):
        return jnp.concatenate(
            [_dot_sel_r2(x[:, j * LANES:(j + 1) * LANES], blk) for j in range(BRANCH // LANES)], axis=1)

    u = u0 + (u_nbr - u0) * mu_ref[d:d + 1, :]
    wa = wa0 + (wa_nbr - wa0) * muw_ref[d:d + 1, :]
    r, k, v = u[:, 0:BRANCH], u[:, BRANCH:2 * BRANCH], u[:, 2 * BRANCH:3 * BRANCH]
    wlin = w0_ref[d:d + 1, :] + _dot_hp(jnp.tanh(wa), w2_ref[d])
    w = -_softplus(-wlin) - 0.5
    lw = -jnp.exp(w)
    eta = jax.nn.sigmoid(a0_ref[...] + _dot_hp(wa, a2_ref[...]))
    kk = k * kk_ref[...]
    kk = kk / jnp.maximum(jnp.sqrt(head_sum(kk * kk)), 1e-12)
    k2 = k * (1.0 + (eta - 1.0) * ka_ref[...])
    bonus = head_sum(r * k2 * rk_ref[...]) * v
    return r, k2, v, lw, -kk, kk * eta, bonus


def _wkv_body(reverse, *refs):
    (u_ref, uh_ref, wa_ref, wah_ref, mu_ref, muw_ref, w0_ref, w2_ref, a0_ref, a2_ref, kk_ref, ka_ref, rk_ref,
     blk_ref) = refs[:14]
    if reverse:
        yf_ref, bonusf_ref, g_ref, lnw_ref, lnb_ref, o_ref, s_ref, scr, wscr = refs[14:]
    else:
        o_ref, bonus_ref, s_ref, scr, wscr = refs[14:]
    ci = pl.program_id(0)

    @pl.when(ci == 0)
    def _():
        s_ref[...] = jnp.zeros_like(s_ref)

    nb, rows = u_ref.shape[0], u_ref.shape[1]
    direction = 1 if reverse else 0
    halo_at = SUBLANES + rows if reverse else 0
    nbr_at = SUBLANES + 1 if reverse else SUBLANES - 1
    blk = blk_ref[...]
    ops = []
    for bi in range(nb):
        scr[bi, halo_at:halo_at + SUBLANES, :] = jnp.where(ci > 0, uh_ref[bi], 0.0)
        wscr[bi, halo_at:halo_at + SUBLANES, :] = jnp.where(ci > 0, wah_ref[bi], 0.0)
        scr[bi, SUBLANES:SUBLANES + rows, :] = u_ref[bi]
        wscr[bi, SUBLANES:SUBLANES + rows, :] = wa_ref[bi]
        ops.append(_rwkv_operands(direction, u_ref[bi], scr[bi, pl.ds(nbr_at, rows), :], wa_ref[bi],
                                  wscr[bi, pl.ds(nbr_at, rows), :], mu_ref, muw_ref, w0_ref, w2_ref, a0_ref, a2_ref,
                                  kk_ref, ka_ref, rk_ref, blk))
    c = min(WKV_CHUNK, rows)
    nsub = rows // c
    c2 = 2 * c
    npair = BRANCH // LANES
    row = lax.broadcasted_iota(jnp.int32, (c, c), 0)
    col = lax.broadcasted_iota(jnp.int32, (c, c), 1)
    tri_bf = jnp.where((col >= row) if reverse else (col <= row), 1.0, 0.0).astype(BF16)
    lane = lax.broadcasted_iota(jnp.int32, (c, LANES), 1)
    first = lane < HEAD64
    row2 = lax.broadcasted_iota(jnp.int32, (c2, c2), 0)
    col2 = lax.broadcasted_iota(jnp.int32, (c2, c2), 1)
    same = jnp.where(row2 >= c, 1, 0) == jnp.where(col2 >= c, 1, 0)
    if reverse:
        strict, incl = same & (col2 > row2), same & (col2 >= row2)
    else:
        strict, incl = same & (col2 < row2), same & (col2 <= row2)
    diag = row2 == col2
    eye2 = jnp.where(diag, 1.0, 0.0)
    zero_blk = jnp.zeros((c2, LANES), F32)
    squarings = max(1, int(math.ceil(math.log2(c))) - 1)

    def stack(x):
        return jnp.concatenate([jnp.where(first, x, 0.0), jnp.where(first, 0.0, x)], axis=0)

    subs = list(range(nsub))[::-1] if reverse else list(range(nsub))
    chains = [(bi, sub, p) for sub in subs for bi in range(nb) for p in range(npair)]
    pre = {}
    for bi, sub in [(bi, sub) for sub in subs for bi in range(nb)]:
        rs = slice(sub * c, (sub + 1) * c)
        r, k, v, lw, a, b = (x[rs, :] for x in ops[bi][:6])
        cum = _dot_sel_l(tri_bf, lw)
        cum_prev = cum - lw
        tot = cum[0:1, :] if reverse else cum[c - 1:c, :]
        mid = cum[c // 2:c // 2 + 1, :]
        e_in = jnp.exp(mid - cum)
        e_end = jnp.exp(tot - cum)
        pre[bi, sub] = dict(am=a * jnp.exp(cum_prev - mid), bm=b * e_in, km=k * e_in, rm=r * jnp.exp(cum - mid),
                            a0=a * jnp.exp(cum_prev), r0=r * jnp.exp(cum), bend=b * e_end, kend=k * e_end,
                            gdiag=jnp.exp(tot), v=v)

    def part(name, ch):
        bi, sub, p = ch
        return pre[bi, sub][name][:, p * LANES:(p + 1) * LANES]

    v_s = [stack(part("v", ch)) for ch in chains]
    prod = [_dot_nt(jnp.concatenate([stack(part("am", ch)), stack(part("rm", ch))], axis=0).astype(BF16),
                    jnp.concatenate([stack(part("bm", ch)), stack(part("km", ch))], axis=0).astype(BF16))
            for ch in chains]
    l_ab = [jnp.where(strict, x[0:c2, 0:c2], 0.0) for x in prod]
    l_ak = [jnp.where(strict, x[0:c2, c2:2 * c2], 0.0) for x in prod]
    a_rb = [jnp.where(incl, x[c2:2 * c2, 0:c2], 0.0) for x in prod]
    a_rk = [jnp.where(incl, x[c2:2 * c2, c2:2 * c2], 0.0) for x in prod]
    tinv = [eye2 + x for x in l_ab]
    pw = [x.astype(BF16) for x in l_ab]
    pw = [_dot(x, x).astype(BF16) for x in pw]
    lakv = [_dot1(x, y) for x, y in zip(l_ak, v_s)]
    for i in range(squarings):
        if i == squarings - 1:
            tinv = [t + _dot(t.astype(BF16), x) for t, x in zip(tinv, pw)]
        else:
            xs = [_dot(jnp.concatenate([t.astype(BF16), x], axis=0), x) for t, x in zip(tinv, pw)]
            tinv = [t + x[0:c2, :] for t, x in zip(tinv, xs)]
            pw = [x[c2:2 * c2, :].astype(BF16) for x in xs]
    au = [_dot(t.astype(BF16), jnp.concatenate([stack(part("a0", ch)), x], axis=1).astype(BF16))
          for t, x, ch in zip(tinv, lakv, chains)]
    big = []
    for n, ch in enumerate(chains):
        ends_t = jnp.concatenate([stack(part("bend", ch)), stack(part("kend", ch))], axis=0).T
        lhs3 = jnp.concatenate([jnp.concatenate([a_rb[n], a_rk[n]], axis=1), ends_t], axis=0).astype(BF16)
        rhs3 = jnp.concatenate([au[n], jnp.concatenate([zero_blk, v_s[n]], axis=1)], axis=0).astype(BF16)
        big.append(_dot(lhs3, rhs3))
    lhs4 = []
    for n, ch in enumerate(chains):
        r_hat = stack(part("r0", ch)) + big[n][0:c2, 0:LANES]
        gmat = jnp.where(diag, part("gdiag", ch), 0.0) + big[n][c2:c2 + LANES, 0:LANES]
        g_hi, g_lo = _split2(gmat)
        lhs4.append((jnp.concatenate([r_hat.astype(BF16), g_hi], axis=0), g_lo))
    ys = {}
    per_sub = nb * npair
    for si, sub in enumerate(subs):
        idx = range(si * per_sub, (si + 1) * per_sub)
        st = [_split2(s_ref[n - si * per_sub]) for n in idx]
        res = [_dot(lhs4[n][0], s_hi) for n, (s_hi, s_lo) in zip(idx, st)]
        fine = [_dot(lhs4[n][0][c2:c2 + LANES, :], s_lo) + _dot(lhs4[n][1], s_hi) for n, (s_hi, s_lo) in zip(idx, st)]
        for n, x, f in zip(idx, res, fine):
            s_ref[n - si * per_sub] = x[c2:c2 + LANES, :] + f + big[n][c2:c2 + LANES, LANES:2 * LANES]
            y_s = x[0:c2, :] + big[n][0:c2, LANES:2 * LANES]
            ys[chains[n]] = y_s[0:c, :] + y_s[c:c2, :]
    if not reverse:
        for bi in range(nb):
            bonus_ref[bi] = ops[bi][6]
    for bi, sub in [(bi, sub) for sub in subs for bi in range(nb)]:
        rs = slice(sub * c, (sub + 1) * c)
        y = jnp.concatenate([ys[bi, sub, p] for p in range(npair)], axis=1)
        if not reverse:
            o_ref[bi, rs, :] = y
            continue
        wkv = y + yf_ref[bi, rs, :]
        inv_n = 1.0 / HEAD64

        def head_mean(x):
            return jnp.concatenate(
                [_dot_sel_r2(x[:, j * LANES:(j + 1) * LANES], blk) for j in range(npair)], axis=1) * inv_n

        cen = wkv - head_mean(wkv)
        gn = cen * lax.rsqrt(head_mean(cen * cen) + RWKV_GN_EPS) * lnw_ref[...] + lnb_ref[...]
        bonus = bonusf_ref[bi, rs, :] + ops[bi][6][rs, :]
        o_ref[bi, rs, :] = ((gn + bonus) * _silu(g_ref[bi, rs, :])).astype(o_ref.dtype)


def _wkv(layer, reverse, proj, prm, extra, bsz, seq):
    c = min(WKV_CHUNK * WKV_SUB, seq)
    nc = seq // c
    npair = BRANCH // LANES
    wide = 3 * BRANCH
    per = c // SUBLANES
    rmap = (lambda ci: nc - 1 - ci) if reverse else (lambda ci: ci)
    if reverse:
        hmap = lambda ci: jnp.minimum((rmap(ci) + 1) * per, seq // SUBLANES - 1)
    else:
        hmap = lambda ci: jnp.maximum(rmap(ci) * per - 1, 0)
    as3d = lambda t: t.reshape(bsz, seq, t.shape[-1])
    proj3 = as3d(proj)
    tile = pl.BlockSpec((bsz, c, BRANCH), lambda ci: (0, rmap(ci), 0))
    full = lambda shape: _layer_spec(shape, layer)
    in_specs = [pl.BlockSpec((bsz, c, wide), lambda ci: (0, rmap(ci), C_RKV // wide)),
                pl.BlockSpec((bsz, SUBLANES, wide), lambda ci: (0, hmap(ci), C_RKV // wide)),
                pl.BlockSpec((bsz, c, LANES), lambda ci: (0, rmap(ci), C_WA // LANES)),
                pl.BlockSpec((bsz, SUBLANES, LANES), lambda ci: (0, hmap(ci), C_WA // LANES)),
                full((2, wide)), full((2, LANES)), full((2, BRANCH)), full((2, LANES, BRANCH)),
                full((1, BRANCH)), full((LANES, BRANCH)), full((1, BRANCH)), full((1, BRANCH)), full((1, BRANCH)),
                pl.BlockSpec((LANES, LANES), lambda ci: (0, 0))]
    args = [proj3, proj3, proj3, proj3, prm["mu_rkv"], prm["mu_wa"], prm["w0"], prm["w2"], prm["a0"], prm["a2"],
            prm["k_k"], prm["k_a"], prm["r_k"], prm["blk"]]
    out_sds = jax.ShapeDtypeStruct((bsz, seq, BRANCH), F32)
    if reverse:
        y_f, bonus_f, ln_w, ln_b = extra
        in_specs += [tile, tile,
                     pl.BlockSpec((bsz, c, BRANCH), lambda ci: (0, rmap(ci), C_RG // BRANCH)),
                     full((1, BRANCH)), full((1, BRANCH))]
        args += [y_f, bonus_f, proj3, ln_w, ln_b]
        out_specs, out_shape = tile, jax.ShapeDtypeStruct((bsz, seq, BRANCH), BF16)
    else:
        out_specs, out_shape = [tile, tile], [out_sds, out_sds]
    return pl.pallas_call(
        functools.partial(_wkv_body, reverse),
        grid=(nc,),
        in_specs=in_specs,
        out_specs=out_specs,
        out_shape=out_shape,
        scratch_shapes=[pltpu.VMEM((bsz * npair, LANES, LANES), F32),
                        pltpu.VMEM((bsz, c + 2 * SUBLANES, wide), F32),
                        pltpu.VMEM((bsz, c + 2 * SUBLANES, LANES), F32)],
        compiler_params=_cparams(("arbitrary",)),
        name="wkv_bwd" if reverse else "wkv_fwd",
    )(*args)


def _rope(x, cos, sin_signed):
    blocks = []
    for j in range(x.shape[1] // LANES):
        xb = x[:, j * LANES:(j + 1) * LANES]
        blocks.append(xb * cos + pltpu.roll(xb, LANES // 2, 1) * sin_signed)
    return jnp.concatenate(blocks, axis=1) if len(blocks) > 1 else blocks[0]


def _attn_prep_body(dq_ref, dk_ref, dv_ref, gq_ref, gk_ref, gv_ref, cd_ref, sd_ref, cg_ref, sg_ref, qw_ref, kw_ref,
                    q1_ref, q2_ref, dko_ref, dvo_ref, gqo_ref, gko_ref, gvo_ref):
    cd, sd, cg, sg = cd_ref[...], sd_ref[...], cg_ref[...], sg_ref[...]
    q = _rope(dq_ref[...], cd, sd) * (HEAD64 ** -0.5 * LOG2E)
    lane = lax.broadcasted_iota(jnp.int32, q.shape, 1)
    comp0 = (lane & (HEAD64 // 2)) == 0
    q1_ref[...] = jnp.where(comp0, q, 0.0).astype(BF16)
    q2_ref[...] = jnp.where(comp0, 0.0, q).astype(BF16)
    dko_ref[...] = _rope(dk_ref[...], cd, sd).astype(BF16)

    def store_vt(v_ref, o_ref):
        vt = v_ref[...].T
        tl = vt.shape[1]
        for h in range(vt.shape[0] // LANES):
            o_ref[0, h, 0:LANES, :] = vt[h * LANES:(h + 1) * LANES, :].astype(BF16)
            o_ref[0, h, LANES:VT_ROWS, :] = jnp.ones((VT_ROWS - LANES, tl), BF16)

    store_vt(dv_ref, dvo_ref)

    def norm_heads(x, w):
        return jnp.concatenate(
            [_rms(x[:, j * LANES:(j + 1) * LANES], w) for j in range(x.shape[1] // LANES)], axis=1)

    gq = _rope(norm_heads(gq_ref[...], qw_ref[...]), cg, sg) * (LANES ** -0.5 * LOG2E)
    gqo_ref[...] = gq.astype(BF16)
    gko_ref[...] = _rope(norm_heads(gk_ref[...], kw_ref[...]), cg, sg).astype(BF16)
    store_vt(gv_ref, gvo_ref)


def _attn_prep(layer, proj, tabs, qw, kw, bsz, seq):
    tl = min(512, seq)
    nt = seq // tl
    m = bsz * seq
    colspec = lambda width, off: pl.BlockSpec((tl, width), lambda b, i: (b * nt + i, off // width))
    tab = pl.BlockSpec((tl, LANES), lambda b, i: (i, 0))
    wspec = _layer_spec((1, LANES), layer)
    o512 = pl.BlockSpec((tl, BRANCH), lambda b, i: (b * nt + i, 0))
    o256 = pl.BlockSpec((tl, 256), lambda b, i: (b * nt + i, 0))
    s512 = jax.ShapeDtypeStruct((m, BRANCH), BF16)
    s256 = jax.ShapeDtypeStruct((m, 256), BF16)
    vt_spec = lambda heads: pl.BlockSpec((1, heads, VT_ROWS, tl), lambda b, i: (b, 0, 0, i))
    vt_shape = lambda heads: jax.ShapeDtypeStruct((bsz, heads, VT_ROWS, seq), BF16)
    return pl.pallas_call(
        _attn_prep_body,
        grid=(bsz, nt),
        in_specs=[colspec(BRANCH, C_DQ), colspec(BRANCH, C_DK), colspec(BRANCH, C_DV), colspec(BRANCH, C_GQ),
                  colspec(256, C_GK), colspec(256, C_GV), tab, tab, tab, tab, wspec, wspec],
        out_specs=[o512, o512, o512, vt_spec(4), o512, o256, vt_spec(2)],
        out_shape=[s512, s512, s512, vt_shape(4), s512, s256, vt_shape(2)],
        compiler_params=_cparams(("parallel", "parallel")),
        name="attn_prep",
    )(proj, proj, proj, proj, proj, proj, *tabs, qw, kw)


def _flash_body(diff, scale_out, ck, *refs):
    if diff:
        qa_ref, qb_ref, k_ref, vt_ref, gate_ref, lam_ref, nw_ref, o_ref, s_scr, p_scr, acc_scr = refs
    else:
        qa_ref, qb_ref, k_ref, vt_ref, gate_ref, o_ref, s_scr, p_scr, acc_scr = refs
    tq = qa_ref.shape[0]
    q = jnp.concatenate([qa_ref[...], qb_ref[...]], axis=0)
    nchunk = k_ref.shape[0] // ck

    def scores(j, slot):
        off = pl.multiple_of(j * ck, ck)
        s = _dot_nt(k_ref[pl.ds(off, ck), :], q)
        s_scr[slot] = s
        return jnp.max(s, axis=0, keepdims=True)

    def pv(j, slot, alpha):
        off = pl.multiple_of(j * ck, ck)
        acc_scr[...] = alpha * acc_scr[...] + _dot(vt_ref[0, 0, :, pl.ds(off, ck)], p_scr[slot])

    def step(j, slot, m_prev, cmax, alpha_prev):
        cmax_next = scores(jnp.minimum(j + 1, nchunk - 1), 1 - slot)
        pv(jnp.maximum(j - 1, 0), 1 - slot, alpha_prev)
        m_new = jnp.maximum(m_prev, cmax)
        p_scr[slot] = jnp.exp2(s_scr[slot] - m_new).astype(BF16)
        return m_new, cmax_next, jnp.exp2(m_prev - m_new)

    def body(i, carry):
        m, cmax, alpha = carry
        m, cmax, alpha = step(2 * i, 0, m, cmax, alpha)
        return step(2 * i + 1, 1, m, cmax, alpha)

    acc_scr[...] = jnp.zeros_like(acc_scr)
    p_scr[1] = jnp.zeros(p_scr.shape[1:], BF16)
    init = (jnp.full((1, 2 * tq), -jnp.inf, F32), scores(0, 0), jnp.ones((1, 2 * tq), F32))
    _, _, alpha_last = lax.fori_loop(0, nchunk // 2, body, init)
    pv(nchunk - 1, 1, alpha_last)
    acc = acc_scr[...]
    o = (acc[0:LANES, :] / acc[LANES:LANES + 1, :]).T
    oa, ob = o[0:tq, :], o[tq:2 * tq, :]
    if diff:
        lp = lam_ref[...]
        lam = (jnp.exp(jnp.sum(lp[0:1, :] * lp[1:2, :], axis=1, keepdims=True))
               - jnp.exp(jnp.sum(lp[2:3, :] * lp[3:4, :], axis=1, keepdims=True)) + (1.0 - scale_out))
        out = _rms(oa - lam * ob, nw_ref[...]) * scale_out
    else:
        out = jnp.concatenate([oa, ob], axis=1)
    o_ref[...] = (out * _silu(gate_ref[...])).astype(o_ref.dtype)


def _flash(layer, diff, qa, qb, k, vt, proj, extra, lambda_init, bsz, seq):
    tq = min(FLASH_TQ, seq)
    ck = min(FLASH_CK, seq // 2)
    assert seq % (2 * ck) == 0 and seq % tq == 0
    nq = seq // tq
    groups = k.shape[1] // LANES
    out_w = LANES if diff else 2 * LANES
    gate_col = (C_DG if diff else C_GG) // out_w
    qa_spec = pl.BlockSpec((tq, LANES), lambda b, g, i: (b * nq + i, g if diff else 2 * g))
    qb_spec = pl.BlockSpec((tq, LANES), lambda b, g, i: (b * nq + i, g if diff else 2 * g + 1))
    k_spec = pl.BlockSpec((seq, LANES), lambda b, g, i: (b, g))
    vt_spec = pl.BlockSpec((1, 1, VT_ROWS, seq), lambda b, g, i: (b, g, 0, 0))
    gate_spec = pl.BlockSpec((tq, out_w), lambda b, g, i: (b * nq + i, gate_col + g))
    in_specs = [qa_spec, qb_spec, k_spec, vt_spec, gate_spec]
    args = [qa, qb, k, vt, proj]
    if diff:
        in_specs += [_layer_spec((4, HEAD64), layer), _layer_spec((1, LANES), layer)]
        args += list(extra)
    out_spec = pl.BlockSpec((tq, out_w), lambda b, g, i: (b * nq + i, g))
    return pl.pallas_call(
        functools.partial(_flash_body, diff, 1.0 - lambda_init, ck),
        grid=(bsz, groups, nq),
        in_specs=in_specs,
        out_specs=out_spec,
        out_shape=jax.ShapeDtypeStruct((bsz * seq, BRANCH), BF16),
        scratch_shapes=[pltpu.VMEM((2, ck, 2 * tq), F32), pltpu.VMEM((2, ck, 2 * tq), BF16),
                        pltpu.VMEM((VT_ROWS, 2 * tq), F32)],
        compiler_params=_cparams(("parallel", "parallel", "parallel")),
        name="diff_attn" if diff else "gqa_attn",
    )(*args)


def _outproj_body(x_ref, ya_ref, yb_ref, yc_ref, yd_ref, w_ref, pw_ref, o_ref):
    mix = (_dot(ya_ref[...], w_ref[0]) + _dot(yb_ref[...], w_ref[1])
           + _dot(yc_ref[...], w_ref[2]) + _dot(yd_ref[...], w_ref[3]))
    o_ref[...] = x_ref[...] + _rms(mix, pw_ref[...])


def _out_proj(layer, x2, y_a, y_b, y_c, y_d, w4, pw):
    m = x2.shape[0]
    tm = min(512, m)
    rows = lambda width: pl.BlockSpec((tm, width), lambda i: (i, 0))
    return pl.pallas_call(
        _outproj_body,
        grid=(m // tm,),
        in_specs=[rows(D_MODEL), rows(BRANCH), rows(BRANCH), rows(BRANCH), rows(BRANCH),
                  _layer_spec((4, BRANCH, D_MODEL), layer), _layer_spec((1, D_MODEL), layer)],
        out_specs=rows(D_MODEL),
        out_shape=jax.ShapeDtypeStruct((m, D_MODEL), F32),
        compiler_params=_cparams(("parallel",)),
        name="out_proj",
    )(x2, y_a, y_b, y_c, y_d, w4, pw)


def _rotary_cols(start, width):
    quarter = HEAD64 // 2
    return tuple((start + blk + q * quarter, quarter) for blk in range(0, width, LANES) for q in (0, 2, 1, 3))


_W_IN_PIECES = (((512, 1024), (0, 512), (1552, 1536), (3216, 512))
                + _rotary_cols(3728, 1024) + ((4752, 1024),)
                + _rotary_cols(5776, 768) + ((6544, 768),)
                + ((3088, 128), (1536, 16)))


def _reorder_body(w_ref, o_ref):
    dst = 0
    for src, width in _W_IN_PIECES:
        o_ref[dst:dst + width, :] = w_ref[src:src + width, :].astype(BF16)
        dst += width
    o_ref[dst:N_PROJ, :] = jnp.zeros((N_PROJ - dst, o_ref.shape[1]), BF16)


def _reorder_w_in(w):
    w_t = jnp.swapaxes(w, 1, 2)
    depth, cols, rows = w_t.shape
    tc = 256
    return pl.pallas_call(
        _reorder_body,
        grid=(depth, rows // tc),
        in_specs=[pl.BlockSpec((None, cols, tc), lambda l, i: (l, 0, i))],
        out_specs=pl.BlockSpec((None, N_PROJ, tc), lambda l, i: (l, 0, i)),
        out_shape=jax.ShapeDtypeStruct((depth, N_PROJ, rows), BF16),
        compiler_params=_cparams(("parallel", "parallel")),
        name="w_in_layout",
    )(w_t)


def _rope_tables(seq):
    half = HEAD64 // 2
    inv = (np.float32(ROPE_THETA) ** (-np.arange(half, dtype=np.float32) / np.float32(half))).astype(np.float32)
    t = np.arange(seq, dtype=np.int32)

    def tab(pos):
        ang = (pos.astype(np.float32)[:, None] * inv[None, :]).astype(np.float64)
        return np.cos(ang).astype(np.float32), np.sin(ang).astype(np.float32)

    def lanes(g1, g2):
        (c1, s1), (c2, s2) = g1, g2
        return np.concatenate([c1, c2, c1, c2], axis=1), np.concatenate([-s1, -s2, s1, s2], axis=1)

    p, r, c = tab(t), tab(t // GRID_W), tab(t % GRID_W)
    return tuple(jnp.asarray(a) for a in lanes(p, p) + lanes(r, c))


def _pad_lanes(v, width=LANES):
    v = v.reshape(v.shape[0], 1, -1)
    return jnp.pad(v, ((0, 0), (0, 0), (0, width - v.shape[2])))


def kernel(x, pre_norm_w, post_norm_w, w_in, w_out, conv_w, conv_b, ssm_a_log, ssm_dt_bias, ssm_d, ssm_norm_w,
           rwkv_mu, rwkv_w0, rwkv_w2, rwkv_a0, rwkv_a2, rwkv_k_k, rwkv_k_a, rwkv_r_k, rwkv_ln_w, rwkv_ln_b,
           diff_lambda, diff_norm_w, gqa_q_norm_w, gqa_k_norm_w):
    bsz, seq, _ = x.shape
    depth = w_in.shape[0]
    m = bsz * seq
    x2 = x.reshape(m, D_MODEL)
    tabs = _rope_tables(seq)
    lane_head = jnp.arange(LANES) // HEAD64
    blk = (lane_head[:, None] == lane_head[None, :]).astype(BF16)
    head_of_lane = jnp.arange(BRANCH) // HEAD64
    sels = [(jnp.arange(LANES)[:, None] == (d * SSM_HEADS + head_of_lane)[None, :]).astype(BF16) for d in range(2)]
    row = lambda v: v.reshape(depth, 1, -1)
    w_in_r = _reorder_w_in(w_in)
    w_out4 = w_out.reshape(depth, 4, BRANCH, D_MODEL).astype(BF16)
    pre_w, post_w = row(pre_norm_w), row(post_norm_w)
    conv_w8 = jnp.pad(conv_w, ((0, 0), (0, SUBLANES - D_CONV), (0, 0)))
    conv_b1 = row(conv_b)
    dt_bias, a_log = _pad_lanes(ssm_dt_bias), _pad_lanes(ssm_a_log)
    d_skip = row(jnp.repeat(ssm_d, HEAD64, axis=-1))
    ssm_nw = row(ssm_norm_w)
    prm = {
        "mu_rkv": rwkv_mu[:, :, :3 * BRANCH], "mu_wa": rwkv_mu[:, :, 3 * BRANCH:], "w0": rwkv_w0,
        "w2": jnp.pad(rwkv_w2, ((0, 0), (0, 0), (0, LANES - RWKV_RANK), (0, 0))),
        "a0": row(rwkv_a0), "a2": jnp.pad(rwkv_a2, ((0, 0), (LANES - RWKV_RANK, 0), (0, 0))),
        "k_k": row(rwkv_k_k), "k_a": row(rwkv_k_a), "r_k": row(rwkv_r_k), "blk": blk,
    }
    ln_w, ln_b = row(rwkv_ln_w), row(rwkv_ln_b)
    rotary_order = lambda v: row(jnp.swapaxes(v.reshape(depth, 2, 2, HEAD64 // 2), 1, 2))
    diff_nw, gqa_qw, gqa_kw = row(diff_norm_w), rotary_order(gqa_q_norm_w), rotary_order(gqa_k_norm_w)
    for i in range(depth):
        lambda_init = 0.8 - 0.6 * math.exp(-0.3 * i)
        proj = _in_proj(i, x2, pre_w, w_in_r)
        xbc = _conv(i, proj, conv_w8, conv_b1, bsz, seq)
        y_f = _ssd(i, False, xbc, proj, dt_bias, a_log, sels[0], None, bsz, seq)
        y_a = _ssd(i, True, xbc, proj, dt_bias, a_log, sels[1], (y_f, d_skip, ssm_nw), bsz, seq)
        wkv_f, bonus_f = _wkv(i, False, proj, prm, None, bsz, seq)
        y_b = _wkv(i, True, proj, prm, (wkv_f, bonus_f, ln_w, ln_b), bsz, seq).reshape(m, BRANCH)
        q1, q2, dk, dv, gq, gk, gv = _attn_prep(i, proj, tabs, gqa_qw, gqa_kw, bsz, seq)
        y_c = _flash(i, True, q1, q2, dk, dv, proj, (diff_lambda, diff_nw), lambda_init, bsz, seq)
        y_d = _flash(i, False, gq, gq, gk, gv, proj, None, lambda_init, bsz, seq)
        x2 = _out_proj(i, x2, y_a, y_b, y_c, y_d, w_out4, post_w)
    return x2.reshape(bsz, seq, D_MODEL)
```
